```python
import jax, jax.numpy as jnp
from jax import lax
import numpy as np

D_MODEL = 2048
BATCH = 16
SEQ = 256
DEPTH = 2
DEC_BATCH = 4
DEC_SEQ = 4096
PAST_LEN = 512

GRID_W = 64
ROPE_THETA = 10000.0
EPS = 1e-6
BLOCK_Q = 128
MLA_HEADS = 16
MLA_Q_RANK = 512
MLA_KV_RANK = 256
MLA_NOPE = 128
MLA_ROPE = 64
MLA_V = 128
MLA_SCALE = (MLA_NOPE + MLA_ROPE) ** -0.5
GQA_HEADS = 16
GQA_KV_HEADS = 4
GQA_HEAD_DIM = 128
GQA_SCALE = GQA_HEAD_DIM ** -0.5
N_EXPERTS = 16
EC_FACTOR = 2
D_FF_EXPERT = 1024
OFF_CQ = MLA_Q_RANK
OFF_CKV = OFF_CQ + MLA_KV_RANK
OFF_KR = OFF_CKV + MLA_ROPE
OFF_GQ = OFF_KR + GQA_HEADS * GQA_HEAD_DIM
OFF_GK = OFF_GQ + GQA_KV_HEADS * GQA_HEAD_DIM
OFF_GV = OFF_GK + GQA_KV_HEADS * GQA_HEAD_DIM
IN_COLS = OFF_GV + 2 * D_MODEL

kernel_name = "hybrid_mla_gqa_ec_diffusion_step"


def rmsnorm(x, g):
    xf = x.astype(jnp.float32)
    y = xf * lax.rsqrt(jnp.mean(xf * xf, axis=-1, keepdims=True) + EPS)
    return (y * g.astype(jnp.float32)).astype(x.dtype)


def axial_rope_tables(n_tokens, rot_dim, dtype):
    t = jnp.arange(n_tokens)
    row = (t // GRID_W).astype(jnp.float32)
    col = (t % GRID_W).astype(jnp.float32)
    quarter = rot_dim // 4
    inv = ROPE_THETA ** (-jnp.arange(quarter, dtype=jnp.float32) / quarter)
    ang = jnp.concatenate([row[:, None] * inv, col[:, None] * inv], axis=-1)
    return jnp.cos(ang).astype(dtype), jnp.sin(ang).astype(dtype)


def apply_axial_rope(x, cos, sin):
    lead = x.shape[:-1]
    r = x.shape[-1]
    q = r // 4
    xr = x.reshape(lead + (2, 2, q))
    x1, x2 = xr[..., 0, :], xr[..., 1, :]
    bshape = (cos.shape[0],) + (1,) * (x.ndim - 3) + (2, q)
    c = cos.reshape(bshape)
    s = sin.reshape(bshape)
    out = jnp.stack([x1 * c - x2 * s, x1 * s + x2 * c], axis=-2)
    return out.reshape(lead + (r,))


def block_attention(q, k, v, scale):
    b, t, h, dk = q.shape
    kvh = k.shape[2]
    g = h // kvh
    dv = v.shape[-1]
    nb = t // BLOCK_Q
    qb = q.reshape(b, nb, BLOCK_Q, kvh, g, dk).transpose(1, 0, 2, 3, 4, 5)

    def one_block(qi):
        s = jnp.einsum('bqkgd,bskd->bkgqs', qi, k, preferred_element_type=jnp.float32) * scale
        p = jax.nn.softmax(s, axis=-1).astype(v.dtype)
        return jnp.einsum('bkgqs,bskd->bqkgd', p, v)

    o = lax.map(one_block, qb)
    return o.transpose(1, 0, 2, 3, 4, 5).reshape(b, t, h * dv)


def branch_inputs(h, w_in, b_gate, g_mla_q, w_mla_uq, g_mla_kv, g_gqa_q, g_gqa_k):
    b, t, _ = h.shape
    z = h @ w_in
    cq, ckv, krope, qg, kg, vg, gl = jnp.split(z, [OFF_CQ, OFF_CKV, OFF_KR, OFF_GQ, OFF_GK, OFF_GV], axis=-1)
    q_mla = (rmsnorm(cq, g_mla_q) @ w_mla_uq).reshape(b, t, MLA_HEADS, MLA_NOPE + MLA_ROPE)
    ckv = rmsnorm(ckv, g_mla_kv)
    q_g = rmsnorm(qg.reshape(b, t, GQA_HEADS, GQA_HEAD_DIM), g_gqa_q)
    k_g = rmsnorm(kg.reshape(b, t, GQA_KV_HEADS, GQA_HEAD_DIM), g_gqa_k)
    v_g = vg.reshape(b, t, GQA_KV_HEADS, GQA_HEAD_DIM)
    gates = jax.nn.sigmoid(gl + b_gate)
    return q_mla, ckv, krope, q_g, k_g, v_g, gates


def merged_mixers(q_mla, ckv_all, krope_all, q_g, k_all, v_all, gates, w_mla_uk, w_mla_uv, w_out):
    b, s, _ = ckv_all.shape
    k_nope = (ckv_all @ w_mla_uk).reshape(b, s, MLA_HEADS, MLA_NOPE)
    k_rope = jnp.broadcast_to(krope_all[:, :, None, :], (b, s, MLA_HEADS, MLA_ROPE))
    k_mla = jnp.concatenate([k_nope, k_rope], axis=-1)
    v_mla = (ckv_all @ w_mla_uv).reshape(b, s, MLA_HEADS, MLA_V)
    o_a = block_attention(q_mla, k_mla, v_mla, MLA_SCALE)
    o_b = block_attention(q_g, k_all, v_all, GQA_SCALE)
    g_a, g_b = jnp.split(gates, 2, axis=-1)
    return (g_a * o_a + g_b * o_b) @ w_out


def expert_choice_ffn(h, w_router, w_gate_e, w_up_e, w_down_e):
    b, t, d = h.shape
    cap = EC_FACTOR * t // N_EXPERTS
    aff = jax.nn.softmax((h @ w_router).astype(jnp.float32), axis=-1)
    g, idx = lax.top_k(jnp.swapaxes(aff, 1, 2), cap)
    xs = jax.vmap(lambda hb, ib: hb[ib])(h, idx)
    hid = jax.nn.silu(jnp.einsum('becd,edf->becf', xs, w_gate_e)) * jnp.einsum('becd,edf->becf', xs, w_up_e)
    ye = jnp.einsum('becf,efd->becd', hid, w_down_e) * g[..., None].astype(h.dtype)
    return jax.vmap(lambda yb, ib: jnp.zeros((t, d), yb.dtype).at[ib.reshape(-1)].add(yb.reshape(-1, d)))(ye, idx)


def setup_inputs(seed: int = 0) -> dict:
    key = jax.random.key(seed)
    ks = iter(jax.random.split(key, 40))

    def nrm(shape, s):
        return jax.random.normal(next(ks), shape, jnp.float32) * s

    def gain(shape):
        return 1.0 + nrm(shape, 0.05)

    d, L = D_MODEL, DEPTH
    return {
        "x_prompt": nrm((BATCH, SEQ, d), 1.0),
        "x_sample": nrm((DEC_BATCH, DEC_SEQ, d), 1.0),
        "c": nrm((DEC_BATCH, d), 1.0),
        "cache_mla_ckv": nrm((DEC_BATCH, L, PAST_LEN, MLA_KV_RANK), 1.0),
        "cache_mla_krope": nrm((DEC_BATCH, L, PAST_LEN, MLA_ROPE), 1.0),
        "cache_gqa_k": nrm((DEC_BATCH, L, PAST_LEN, GQA_KV_HEADS, GQA_HEAD_DIM), 1.0),
        "cache_gqa_v": nrm((DEC_BATCH, L, PAST_LEN, GQA_KV_HEADS, GQA_HEAD_DIM), 1.0),
        "c_ctx": nrm((d,), 1.0),
        "w_ada": nrm((L, d, 6 * d), 0.5 * d ** -0.5),
        "b_ada": nrm((L, 6 * d), 0.02),
        "g_attn_norm": gain((L, d)),
        "w_in": nrm((L, d, IN_COLS), d ** -0.5),
        "b_gate": nrm((L, 2 * d), 0.02),
        "g_mla_q": gain((L, MLA_Q_RANK)),
        "w_mla_uq": nrm((L, MLA_Q_RANK, MLA_HEADS * (MLA_NOPE + MLA_ROPE)), MLA_Q_RANK ** -0.5),
        "g_mla_kv": gain((L, MLA_KV_RANK)),
        "w_mla_uk": nrm((L, MLA_KV_RANK, MLA_HEADS * MLA_NOPE), MLA_KV_RANK ** -0.5),
        "w_mla_uv": nrm((L, MLA_KV_RANK, MLA_HEADS * MLA_V), MLA_KV_RANK ** -0.5),
        "g_gqa_q": gain((L, GQA_HEAD_DIM)),
        "g_gqa_k": gain((L, GQA_HEAD_DIM)),
        "w_out": nrm((L, d, d), d ** -0.5),
        "g_ffn_norm": gain((L, d)),
        "w_router": nrm((L, d, N_EXPERTS), d ** -0.5),
        "w_gate_e": nrm((L, N_EXPERTS, d, D_FF_EXPERT), d ** -0.5),
        "w_up_e": nrm((L, N_EXPERTS, d, D_FF_EXPERT), d ** -0.5),
        "w_down_e": nrm((L, N_EXPERTS, D_FF_EXPERT, d), D_FF_EXPERT ** -0.5),
        "g_final": gain((d,)),
    }


def reference(x_prompt, x_sample, c, cache_mla_ckv, cache_mla_krope, cache_gqa_k, cache_gqa_v,
              c_ctx, w_ada, b_ada, g_attn_norm, w_in, b_gate, g_mla_q, w_mla_uq, g_mla_kv,
              w_mla_uk, w_mla_uv, g_gqa_q, g_gqa_k, w_out, g_ffn_norm, w_router,
              w_gate_e, w_up_e, w_down_e, g_final):
    xp = x_prompt
    st_ckv, st_krope, st_k, st_v = [], [], [], []
    for l in range(DEPTH):
        mod = jax.nn.silu(c_ctx) @ w_ada[l] + b_ada[l]
        sh1, sc1, gt1, sh2, sc2, gt2 = jnp.split(mod, 6, axis=-1)
        h = rmsnorm(xp, g_attn_norm[l]) * (1 + sc1) + sh1
        q_mla, ckv, krope, q_g, k_g, v_g, gates = branch_inputs(
            h, w_in[l], b_gate[l], g_mla_q[l], w_mla_uq[l], g_mla_kv[l], g_gqa_q[l], g_gqa_k[l])
        st_ckv.append(ckv)
        st_krope.append(krope)
        st_k.append(k_g)
        st_v.append(v_g)
        xp = xp + gt1 * merged_mixers(q_mla, ckv, krope, q_g, k_g, v_g, gates,
                                      w_mla_uk[l], w_mla_uv[l], w_out[l])
        h = rmsnorm(xp, g_ffn_norm[l]) * (1 + sc2) + sh2
        xp = xp + gt2 * expert_choice_ffn(h, w_router[l], w_gate_e[l], w_up_e[l], w_down_e[l])
    y_prompt = rmsnorm(xp, g_final)
    new_mla_ckv = jnp.stack(st_ckv, axis=1)
    new_mla_krope = jnp.stack(st_krope, axis=1)
    new_gqa_k = jnp.stack(st_k, axis=1)
    new_gqa_v = jnp.stack(st_v, axis=1)

    xs = x_sample
    n_lat = xs.shape[1]
    cos_m, sin_m = axial_rope_tables(n_lat, MLA_ROPE, xs.dtype)
    cos_g, sin_g = axial_rope_tables(n_lat, GQA_HEAD_DIM, xs.dtype)
    for l in range(DEPTH):
        mod = (jax.nn.silu(c) @ w_ada[l] + b_ada[l])[:, None, :]
        sh1, sc1, gt1, sh2, sc2, gt2 = jnp.split(mod, 6, axis=-1)
        h = rmsnorm(xs, g_attn_norm[l]) * (1 + sc1) + sh1
        q_mla, ckv, krope, q_g, k_g, v_g, gates = branch_inputs(
            h, w_in[l], b_gate[l], g_mla_q[l], w_mla_uq[l], g_mla_kv[l], g_gqa_q[l], g_gqa_k[l])
        q_mla = jnp.concatenate([q_mla[..., :MLA_NOPE],
                                 apply_axial_rope(q_mla[..., MLA_NOPE:], cos_m, sin_m)], axis=-1)
        krope = apply_axial_rope(krope, cos_m, sin_m)
        q_g = apply_axial_rope(q_g, cos_g, sin_g)
        k_g = apply_axial_rope(k_g, cos_g, sin_g)
        ckv_all = jnp.concatenate([cache_mla_ckv[:, l], ckv], axis=1)
        krope_all = jnp.concatenate([cache_mla_krope[:, l], krope], axis=1)
        k_all = jnp.concatenate([cache_gqa_k[:, l], k_g], axis=1)
        v_all = jnp.concatenate([cache_gqa_v[:, l], v_g], axis=1)
        xs = xs + gt1 * merged_mixers(q_mla, ckv_all, krope_all, q_g, k_all, v_all, gates,
                                      w_mla_uk[l], w_mla_uv[l], w_out[l])
        h = rmsnorm(xs, g_ffn_norm[l]) * (1 + sc2) + sh2
        xs = xs + gt2 * expert_choice_ffn(h, w_router[l], w_gate_e[l], w_up_e[l], w_down_e[l])
    y_sample = rmsnorm(xs, g_final)

    return (y_prompt, y_sample, new_mla_ckv, new_mla_krope, new_gqa_k, new_gqa_v)
```

```python
import functools

import jax
import jax.numpy as jnp
from jax import lax
from jax.experimental import pallas as pl
from jax.experimental.pallas import tpu as pltpu

F32 = jnp.float32
BF16 = jnp.bfloat16

D_MODEL = 2048
GRID_W = 64
ROPE_THETA = 10000.0
EPS = 1e-6
MLA_HEADS = 16
MLA_Q_RANK = 512
MLA_KV_RANK = 256
MLA_NOPE = 128
MLA_ROPE = 64
MLA_V = 128
MLA_QK = MLA_NOPE + MLA_ROPE
MLA_SCALE = MLA_QK ** -0.5
GQA_HEADS = 16
GQA_KV_HEADS = 4
GQA_HEAD_DIM = 128
GQA_SCALE = GQA_HEAD_DIM ** -0.5
N_EXPERTS = 16
EC_FACTOR = 2
D_FF = 1024
LOG2E = 1.4426950408889634

LANES = 128
MXU_DIM = 256
MLA_QK_PAD = MXU_DIM
VMEM_LIMIT = 56 * 1024 * 1024
COMBINE_TM = 128

C_CQ = 0
C_CKV = C_CQ + MLA_Q_RANK
C_KR = C_CKV + MLA_KV_RANK
C_GQ = C_KR + LANES
C_GK = C_GQ + GQA_HEADS * GQA_HEAD_DIM
C_GV = C_GK + GQA_KV_HEADS * GQA_HEAD_DIM
C_END = C_GV + GQA_KV_HEADS * GQA_HEAD_DIM


def _dot(a, b):
    return jnp.dot(a, b, preferred_element_type=F32)


def _dot_nt(a, b):
    return lax.dot_general(a, b, (((1,), (1,)), ((), ())), preferred_element_type=F32)


def _split(x):
    hi = x.astype(BF16)
    lo = (x - hi.astype(F32)).astype(BF16)
    return hi, lo


def _rms(x):
    return x * lax.rsqrt(jnp.mean(x * x, axis=-1, keepdims=True) + EPS)


def _params(sem):
    return pltpu.CompilerParams(dimension_semantics=sem, vmem_limit_bytes=VMEM_LIMIT)


def _const_spec(shape):
    nd = len(shape)
    return pl.BlockSpec(shape, lambda *_: (0,) * nd, pipeline_mode=pl.Buffered(1))


def _ada_kernel(c_ref, w_ref, b_ref, o_ref):
    c = c_ref[...]
    s = c / (1.0 + jnp.exp(-c))
    s_hi, s_lo = _split(s)
    w_hi, w_lo = _split(w_ref[...])
    o_ref[...] = _dot(s_hi, w_hi) + _dot(s_hi, w_lo) + _dot(s_lo, w_hi) + b_ref[...]


def _ada(cc, w_ada, b_ada):
    n_layers, d, n6 = w_ada.shape
    rows = cc.shape[0]
    tn = 1024
    return pl.pallas_call(
        _ada_kernel,
        grid=(n_layers, n6 // tn),
        in_specs=[
            pl.BlockSpec((rows, d), lambda l, j: (0, 0)),
            pl.BlockSpec((None, d, tn), lambda l, j: (l, 0, j)),
            pl.BlockSpec((None, 1, tn), lambda l, j: (l, 0, j)),
        ],
        out_specs=pl.BlockSpec((None, rows, tn), lambda l, j: (l, 0, j)),
        out_shape=jax.ShapeDtypeStruct((n_layers, rows, n6), F32),
        compiler_params=_params(("arbitrary", "arbitrary")),
        name="ada",
    )(cc, w_ada, b_ada.reshape(n_layers, 1, n6))


def _swap_halves(x, q):
    lane = lax.broadcasted_iota(jnp.int32, x.shape, 1)
    first = (lane % (2 * q)) < q
    return jnp.where(first, pltpu.roll(x, LANES - q, 1), pltpu.roll(x, q, 1))


def _qkv_kernel(rope, cache_out, x_ref, sh_ref, sc_ref, gn_ref, w_ref, gq_ref, wuq_ref, gkv_ref,
                ggq_ref, ggk_ref, cm_ref, sm_ref, cg_ref, sg_ref, *outs):
    qm_ref, ckv_ref, kr_ref, qg_ref, kg_ref, vg_ref = outs[:6]
    x = x_ref[...]
    h = (_rms(x) * gn_ref[...] * (1.0 + sc_ref[...]) + sh_ref[...]).astype(BF16)

    cq = _rms(_dot(h, w_ref[:, C_CQ:C_CKV])) * gq_ref[...]
    qm = _dot(cq.astype(BF16), wuq_ref[...]) * (MLA_SCALE * LOG2E)
    for hd in range(MLA_HEADS):
        lo = hd * MLA_QK_PAD
        qm_ref[:, lo:lo + LANES] = qm[:, lo:lo + LANES].astype(BF16)
        t = qm[:, lo + LANES:lo + 2 * LANES]
        if rope:
            t = t * cm_ref[...] + _swap_halves(t, MLA_ROPE // 4) * sm_ref[...]
        qm_ref[:, lo + LANES:lo + 2 * LANES] = t.astype(BF16)

    ckv = _rms(_dot(h, w_ref[:, C_CKV:C_KR])) * gkv_ref[...]
    ckv_ref[...] = ckv.astype(BF16)
    kr = _dot(h, w_ref[:, C_KR:C_GQ])
    if cache_out:
        outs[6][...] = ckv
        outs[7][...] = kr[:, :MLA_ROPE]
    if rope:
        kr = kr * cm_ref[...] + _swap_halves(kr, MLA_ROPE // 4) * sm_ref[...]
    kr_ref[...] = kr.astype(BF16)

    zq = _dot(h, w_ref[:, C_GQ:C_GK])
    for hd in range(GQA_HEADS):
        lo = hd * GQA_HEAD_DIM
        t = _rms(zq[:, lo:lo + GQA_HEAD_DIM]) * ggq_ref[...]
        if rope:
            t = t * cg_ref[...] + _swap_halves(t, GQA_HEAD_DIM // 4) * sg_ref[...]
        qg_ref[:, lo:lo + GQA_HEAD_DIM] = (t * (GQA_SCALE * LOG2E)).astype(BF16)
    zk = _dot(h, w_ref[:, C_GK:C_GV])
    for hd in range(GQA_KV_HEADS):
        lo = hd * GQA_HEAD_DIM
        t = _rms(zk[:, lo:lo + GQA_HEAD_DIM]) * ggk_ref[...]
        if cache_out:
            outs[8][:, lo:lo + GQA_HEAD_DIM] = t
        if rope:
            t = t * cg_ref[...] + _swap_halves(t, GQA_HEAD_DIM // 4) * sg_ref[...]
        kg_ref[:, lo:lo + GQA_HEAD_DIM] = t.astype(BF16)
    zv = _dot(h, w_ref[:, C_GV:C_END])
    vg_ref[...] = zv.astype(BF16)
    if cache_out:
        outs[9][...] = zv


def _qkv(x, modr, mod_row, lw, tabs, *, rope, cache_out, n_lat):
    n, d = x.shape
    tm = min(256, n)
    pos_tiles = max(n_lat // tm, 1)
    kvw = GQA_KV_HEADS * GQA_HEAD_DIM
    row = lambda w: pl.BlockSpec((tm, w), lambda i: (i, 0))
    mod = lambda j: pl.BlockSpec((None, 1, d), lambda i: (mod_row(i * tm) * 6 + j, 0, 0))
    tab = lambda w: pl.BlockSpec((tm, w), lambda i: (i % pos_tiles, 0))
    in_specs = [
        row(d), mod(0), mod(1), _const_spec((1, d)), _const_spec((d, C_END)),
        _const_spec((1, MLA_Q_RANK)), _const_spec((MLA_Q_RANK, MLA_HEADS * MLA_QK_PAD)),
        _const_spec((1, MLA_KV_RANK)), _const_spec((1, GQA_HEAD_DIM)), _const_spec((1, GQA_HEAD_DIM)),
        tab(LANES), tab(LANES), tab(LANES), tab(LANES),
    ]
    out_shape = [
        jax.ShapeDtypeStruct((n, MLA_HEADS * MLA_QK_PAD), BF16),
        jax.ShapeDtypeStruct((n, MLA_KV_RANK), BF16),
        jax.ShapeDtypeStruct((n, LANES), BF16),
        jax.ShapeDtypeStruct((n, d), BF16),
        jax.ShapeDtypeStruct((n, kvw), BF16),
        jax.ShapeDtypeStruct((n, kvw), BF16),
    ]
    out_specs = [row(MLA_HEADS * MLA_QK_PAD), row(MLA_KV_RANK), row(LANES), row(d), row(kvw), row(kvw)]
    if cache_out:
        out_shape += [
            jax.ShapeDtypeStruct((n, MLA_KV_RANK), F32),
            jax.ShapeDtypeStruct((n, MLA_ROPE), F32),
            jax.ShapeDtypeStruct((n, kvw), F32),
            jax.ShapeDtypeStruct((n, kvw), F32),
        ]
        out_specs += [row(MLA_KV_RANK), row(MLA_ROPE), row(kvw), row(kvw)]
    return pl.pallas_call(
        functools.partial(_qkv_kernel, rope, cache_out),
        grid=(n // tm,),
        in_specs=in_specs,
        out_specs=out_specs,
        out_shape=out_shape,
        compiler_params=_params(("arbitrary",)),
        name="qkv_rope" if rope else "qkv_ctx",
    )(x, modr, modr, lw["g_attn"], lw["w_a"], lw["g_mla_q"], lw["w_uq"], lw["g_mla_kv"],
      lw["g_gqa_q"], lw["g_gqa_k"], *tabs)


def _kv_expand_kernel(ckv_ref, kr_ref, wuk_ref, wuv_ref, k_ref, v_ref):
    ckv = ckv_ref[...]
    kn = _dot(ckv, wuk_ref[...]).astype(BF16)
    kr = kr_ref[...]
    for hd in range(MLA_HEADS):
        k_ref[:, hd * MLA_QK_PAD:hd * MLA_QK_PAD + LANES] = kn[:, hd * MLA_NOPE:(hd + 1) * MLA_NOPE]
        k_ref[:, hd * MLA_QK_PAD + LANES:(hd + 1) * MLA_QK_PAD] = kr
    v_ref[...] = _dot(ckv, wuv_ref[...]).astype(BF16)


def _kv_expand(ckv, kr, w_uk, w_uv):
    b, s, _ = ckv.shape
    ts = min(512, s)
    kw, vw = MLA_HEADS * MLA_QK_PAD, MLA_HEADS * MLA_V
    blk = lambda w: pl.BlockSpec((None, ts, w), lambda i, j: (i, j, 0))
    return pl.pallas_call(
        _kv_expand_kernel,
        grid=(b, s // ts),
        in_specs=[blk(MLA_KV_RANK), blk(LANES), _const_spec((MLA_KV_RANK, MLA_HEADS * MLA_NOPE)),
                  _const_spec((MLA_KV_RANK, vw))],
        out_specs=[blk(kw), blk(vw)],
        out_shape=[jax.ShapeDtypeStruct((b, s, kw), BF16), jax.ShapeDtypeStruct((b, s, vw), BF16)],
        compiler_params=_params(("arbitrary", "arbitrary")),
        name="kv_expand",
    )(ckv, kr, w_uk, w_uv)


def _attn_kernel(group, dq, tk, q_ref, k_ref, v_ref, o_ref, s_ref, vx_ref):
    s_len = k_ref.shape[0]
    tq = q_ref.shape[0]
    dv = v_ref.shape[1]
    n_chunks = s_len // tk

    @pl.when(pl.program_id(2) == 0)
    def _():
        vx_ref[:, :dv] = v_ref[...]
        vx_ref[:, dv:] = jnp.ones((s_len, dv), BF16)

    for g in range(group):
        q = q_ref[:, g * dq:(g + 1) * dq]
        m = jnp.full((tq, LANES), -jnp.inf, F32)
        for c in range(n_chunks):
            s = _dot_nt(q, k_ref[c * tk:(c + 1) * tk, :])
            s_ref[:, c * tk:(c + 1) * tk] = s
            for t in range(tk // LANES):
                m = jnp.maximum(m, s[:, t * LANES:(t + 1) * LANES])
        m = jnp.max(m, axis=1, keepdims=True)
        acc = jnp.zeros((tq, 2 * dv), F32)
        for c in range(n_chunks):
            p = jnp.exp2(s_ref[:, c * tk:(c + 1) * tk] - m).astype(BF16)
            acc = acc + _dot(p, vx_ref[c * tk:(c + 1) * tk, :])
        o_ref[:, g * dv:(g + 1) * dv] = (acc[:, :dv] / acc[:, dv:]).astype(BF16)


def _attn(q, k, v, *, group, dq):
    b, t, _ = q.shape
    s = k.shape[1]
    dv = GQA_HEAD_DIM
    hk = v.shape[2] // dv
    tq = min(512, t)
    tk = min(512, s)
    return pl.pallas_call(
        functools.partial(_attn_kernel, group, dq, tk),
        grid=(b, hk, t // tq),
        in_specs=[
            pl.BlockSpec((None, tq, group * dq), lambda i, h, j: (i, j, h)),
            pl.BlockSpec((None, s, dq), lambda i, h, j: (i, 0, h)),
            pl.BlockSpec((None, s, dv), lambda i, h, j: (i, 0, h)),
        ],
        out_specs=pl.BlockSpec((None, tq, group * dv), lambda i, h, j: (i, j, h)),
        out_shape=jax.ShapeDtypeStruct((b, t, hk * group * dv), BF16),
        scratch_shapes=[pltpu.VMEM((tq, s), F32), pltpu.VMEM((s, 2 * dv), BF16)],
        compiler_params=_params(("arbitrary", "arbitrary", "arbitrary")),
        name="attn",
    )(q, k, v)


def _out_proj_kernel(x_ref, oa_ref, ob_ref, sh1_ref, sc1_ref, gt1_ref, sh2_ref, sc2_ref, gn1_ref,
                     gn2_ref, wgl_ref, bg_ref, wo_ref, wrh_ref, wrl_ref, x1_ref, aff_ref):
    d = x_ref.shape[1]
    x = x_ref[...]
    h = (_rms(x) * gn1_ref[...] * (1.0 + sc1_ref[...]) + sh1_ref[...]).astype(BF16)
    ga = 1.0 / (1.0 + jnp.exp(-(_dot(h, wgl_ref[:, :d]) + bg_ref[:, :d])))
    merged = ga * oa_ref[...].astype(F32)
    gb = 1.0 / (1.0 + jnp.exp(-(_dot(h, wgl_ref[:, d:]) + bg_ref[:, d:])))
    merged = (merged + gb * ob_ref[...].astype(F32)).astype(BF16)
    x1 = x + gt1_ref[...] * _dot(merged, wo_ref[...])
    x1_ref[...] = x1
    h2 = _rms(x1) * gn2_ref[...] * (1.0 + sc2_ref[...]) + sh2_ref[...]
    h_hi, h_lo = _split(h2)
    lg = _dot_nt(wrh_ref[...], h_hi) + _dot_nt(wrh_ref[...], h_lo) + _dot_nt(wrl_ref[...], h_hi)
    e = jnp.exp(lg - jnp.max(lg, axis=0, keepdims=True))
    aff_ref[...] = e / jnp.sum(e, axis=0, keepdims=True)


def _out_proj(x, oa, ob, modr, mod_row, lw):
    n, d = x.shape
    tm = min(256, n)
    row = lambda: pl.BlockSpec((tm, d), lambda i: (i, 0))
    mod = lambda j: pl.BlockSpec((None, 1, d), lambda i: (mod_row(i * tm) * 6 + j, 0, 0))
    return pl.pallas_call(
        _out_proj_kernel,
        grid=(n // tm,),
        in_specs=[row(), row(), row(), mod(0), mod(1), mod(2), mod(3), mod(4),
                  _const_spec((1, d)), _const_spec((1, d)), _const_spec((d, 2 * d)),
                  _const_spec((1, 2 * d)), _const_spec((d, d)),
                  _const_spec((N_EXPERTS, d)), _const_spec((N_EXPERTS, d))],
        out_specs=[row(), pl.BlockSpec((N_EXPERTS, tm), lambda i: (0, i))],
        out_shape=[jax.ShapeDtypeStruct((n, d), F32), jax.ShapeDtypeStruct((N_EXPERTS, n), F32)],
        compiler_params=_params(("arbitrary",)),
        name="out_proj",
    )(x, oa, ob, modr, modr, modr, modr, modr, lw["g_attn"], lw["g_ffn"], lw["w_gl"], lw["b_gate"],
      lw["w_out"], lw["wr_hi"], lw["wr_lo"])


def _lane_cumsum(x_bf16, tri):
    t = x_bf16.shape[1]
    blk = tri.shape[0]
    parts = []
    carry = jnp.zeros((x_bf16.shape[0], 1), F32)
    for c in range(t // blk):
        part = _dot(x_bf16[:, c * blk:(c + 1) * blk], tri) + carry
        carry = part[:, blk - 1:blk]
        parts.append(part)
    return parts[0] if len(parts) == 1 else jnp.concatenate(parts, axis=1)


def _router_kernel(cap, slot_base, slots, aff_ref, idx_ref, cf_ref, cw_ref):
    r = pl.program_id(0)
    a = aff_ref[...]
    n_e, t = a.shape
    bits = pltpu.bitcast(a, jnp.int32)

    thr = jnp.zeros((n_e, 1), jnp.int32)
    for b in range(30, -1, -1):
        cand = thr | (1 << b)
        cnt = jnp.sum((bits >= cand).astype(jnp.int32), axis=1, keepdims=True)
        thr = jnp.where(cnt >= cap, cand, thr)
    gt = bits > thr
    eq = bits == thr
    need = (cap - jnp.sum(gt.astype(jnp.int32), axis=1, keepdims=True)).astype(F32)

    blk = min(MXU_DIM, t)
    ri = lax.broadcasted_iota(jnp.int32, (blk, blk), 0)
    ci = lax.broadcasted_iota(jnp.int32, (blk, blk), 1)
    tri = (ri <= ci).astype(BF16)
    cum_eq = _lane_cumsum(eq.astype(BF16), tri)
    sel = gt | (eq & (cum_eq <= need))
    sel_f = sel.astype(F32)
    cnt_incl = _lane_cumsum(sel.astype(BF16), tri)
    pos = cnt_incl - sel_f

    ts = min(LANES, cap)
    ones = jnp.ones((t, LANES), BF16)
    lane = lax.broadcasted_iota(jnp.int32, (1, LANES), 1)
    for sb in range(cap // ts):
        s_iota = (lax.broadcasted_iota(jnp.int32, (ts, 1), 0) + sb * ts).astype(F32)
        out = jnp.zeros((ts, LANES), F32)
        for e in range(n_e):
            cmp = (cnt_incl[e:e + 1, :] <= s_iota).astype(BF16)
            out = out + jnp.where(lane == e, _dot(cmp, ones), 0.0)
        idx_ref[sb * ts:(sb + 1) * ts, :] = out.astype(jnp.int32) + r * t

    ei = lax.broadcasted_iota(jnp.int32, (n_e, n_e), 0)
    ej = lax.broadcasted_iota(jnp.int32, (n_e, n_e), 1)
    rank = _dot((ej < ei).astype(BF16), sel.astype(BF16))
    e_col = lax.broadcasted_iota(jnp.int32, (n_e, 1), 0).astype(F32)
    flat = e_col * slots + (slot_base + r * cap).astype(F32) + pos
    for k in range(n_e):
        hit = sel & (rank == k)
        cf = jnp.sum(jnp.where(hit, flat, 0.0), axis=0, keepdims=True)
        has = jnp.sum(jnp.where(hit, 1.0, 0.0), axis=0, keepdims=True)
        cf_ref[k:k + 1, :] = jnp.where(has > 0.0, cf, -1.0).astype(jnp.int32)
        cw_ref[k:k + 1, :] = jnp.sum(jnp.where(hit, a, 0.0), axis=0, keepdims=True)


def _router(aff_t, n_req, slot_base, slots):
    n_e, n = aff_t.shape
    t = n // n_req
    cap = EC_FACTOR * t // N_EXPERTS
    return pl.pallas_call(
        functools.partial(_router_kernel, cap, slot_base, slots),
        grid=(n_req,),
        in_specs=[pl.BlockSpec((n_e, t), lambda r: (0, r))],
        out_specs=[pl.BlockSpec((cap, LANES), lambda r: (r, 0)),
                   pl.BlockSpec((n_e, t), lambda r: (0, r)),
                   pl.BlockSpec((n_e, t), lambda r: (0, r))],
        out_shape=[jax.ShapeDtypeStruct((n_req * cap, LANES), jnp.int32),
                   jax.ShapeDtypeStruct((n_e, n), jnp.int32),
                   jax.ShapeDtypeStruct((n_e, n), F32)],
        compiler_params=_params(("arbitrary",)),
        name="router",
    )(aff_t)


def _ffn_kernel(ts, idx_hbm, xp_hbm, xs_hbm, sh_ref, sc_ref, gn_ref, wg_ref, wu_ref, wd_ref,
                ye_ref, idx_smem, xbuf, sem_i, sem_x):
    e = pl.program_id(0)
    j = pl.program_id(1)
    n_tiles = pl.num_programs(1)
    icp = pltpu.make_async_copy(idx_hbm.at[pl.ds(e * n_tiles + j, 1), :], idx_smem, sem_i)
    icp.start()
    icp.wait()

    def gather(src):
        def body(s, carry):
            pltpu.make_async_copy(src.at[pl.ds(idx_smem[0, s], 1), :], xbuf.at[pl.ds(s, 1), :],
                                  sem_x).start()
            return carry
        lax.fori_loop(0, ts, body, 0)

    @pl.when(j == 0)
    def _():
        gather(xp_hbm)

    @pl.when(j > 0)
    def _():
        gather(xs_hbm)

    pltpu.make_async_copy(xs_hbm.at[pl.ds(0, ts), :], xbuf, sem_x).wait()

    h = (_rms(xbuf[...]) * gn_ref[...] * (1.0 + sc_ref[...]) + sh_ref[...]).astype(BF16)
    gate = _dot(h, wg_ref[...])
    hid = (gate / (1.0 + jnp.exp(-gate)) * _dot(h, wu_ref[...])).astype(BF16)
    ye_ref[...] = _dot(hid, wd_ref[...])


def _ffn(idx, x1p, x1s, modr, layer, lw, n_tiles, ts):
    d = x1p.shape[1]
    mod = lambda c: pl.BlockSpec((None, 1, d), lambda e, j: ((layer * 8 + j) * 6 + c, 0, 0))
    return pl.pallas_call(
        functools.partial(_ffn_kernel, ts),
        grid=(N_EXPERTS, n_tiles),
        in_specs=[
            pl.BlockSpec(memory_space=pl.ANY), pl.BlockSpec(memory_space=pl.ANY),
            pl.BlockSpec(memory_space=pl.ANY), mod(3), mod(4),
            pl.BlockSpec((1, d), lambda e, j: (0, 0)),
            pl.BlockSpec((None, d, D_FF), lambda e, j: (e, 0, 0)),
            pl.BlockSpec((None, d, D_FF), lambda e, j: (e, 0, 0)),
            pl.BlockSpec((None, D_FF, d), lambda e, j: (e, 0, 0)),
        ],
        out_specs=pl.BlockSpec((None, ts, d), lambda e, j: (e, j, 0)),
        out_shape=jax.ShapeDtypeStruct((N_EXPERTS, n_tiles * ts, d), F32),
        scratch_shapes=[pltpu.SMEM((1, ts), jnp.int32), pltpu.VMEM((ts, d), F32),
                        pltpu.SemaphoreType.DMA(()), pltpu.SemaphoreType.DMA(())],
        compiler_params=_params(("arbitrary", "arbitrary")),
        name="ffn",
    )(idx, x1p, x1s, modr, modr, lw["g_ffn"], lw["w_gate"], lw["w_up"], lw["w_down"])


def _combine_kernel(final, tm, cf_hbm, ye_hbm, x_ref, cw_ref, gt_ref, gf_ref, o_ref,
                    cf_smem, rows, sem_i, sem_r):
    i = pl.program_id(0)
    n_e = N_EXPERTS
    icp = pltpu.make_async_copy(cf_hbm.at[pl.ds(i, 1), :], cf_smem, sem_i)
    icp.start()
    icp.wait()

    def tok(r, total):
        def cond(st):
            k, _ = st
            return jnp.logical_and(k < n_e, cf_smem[0, r * n_e + jnp.minimum(k, n_e - 1)] >= 0)

        def body(st):
            k, tot = st
            pltpu.make_async_copy(ye_hbm.at[pl.ds(cf_smem[0, r * n_e + k], 1), :],
                                  rows.at[pl.ds(k * tm + r, 1), :], sem_r).start()
            return k + 1, tot + 1

        _, total = lax.while_loop(cond, body, (0, total))
        return total

    total = lax.fori_loop(0, tm, tok, 0)

    def wait_row(_, carry):
        pltpu.make_async_copy(ye_hbm.at[pl.ds(0, 1), :], rows.at[pl.ds(0, 1), :], sem_r).wait()
        return carry

    lax.fori_loop(0, total, wait_row, 0)

    cw = cw_ref[...]
    acc = jnp.zeros(x_ref.shape, F32)
    for k in range(n_e):
        w = cw[:, k:k + 1]
        acc = acc + jnp.where(w > 0.0, w * rows[k * tm:(k + 1) * tm, :], 0.0)
    y = x_ref[...] + gt_ref[...] * acc
    if final:
        y = _rms(y) * gf_ref[...]
    o_ref[...] = y


def _combine(cf, ye, x1, cw, modr, mod_row, g_final, *, final):
    n, d = x1.shape
    tm = COMBINE_TM
    return pl.pallas_call(
        functools.partial(_combine_kernel, final, tm),
        grid=(n // tm,),
        in_specs=[
            pl.BlockSpec(memory_space=pl.ANY), pl.BlockSpec(memory_space=pl.ANY),
            pl.BlockSpec((tm, d), lambda i: (i, 0)),
            pl.BlockSpec((tm, LANES), lambda i: (i, 0)),
            pl.BlockSpec((None, 1, d), lambda i: (mod_row(i * tm) * 6 + 5, 0, 0)),
            pl.BlockSpec((1, d), lambda i: (0, 0)),
        ],
        out_specs=pl.BlockSpec((tm, d), lambda i: (i, 0)),
        out_shape=jax.ShapeDtypeStruct((n, d), F32),
        scratch_shapes=[pltpu.SMEM((1, tm * N_EXPERTS), jnp.int32),
                        pltpu.VMEM((N_EXPERTS * tm, d), F32),
                        pltpu.SemaphoreType.DMA(()), pltpu.SemaphoreType.DMA(())],
        compiler_params=_params(("arbitrary",)),
        name="combine_final" if final else "combine",
    )(cf, ye, x1, cw, modr, g_final)


def _rope_tables(n_tokens, rot_dim):
    t = jnp.arange(n_tokens)
    row = (t // GRID_W).astype(F32)
    col = (t % GRID_W).astype(F32)
    quarter = rot_dim // 4
    inv = ROPE_THETA ** (-jnp.arange(quarter, dtype=F32) / quarter)
    ar, ac = row[:, None] * inv, col[:, None] * inv
    cos = jnp.concatenate([jnp.cos(ar), jnp.cos(ar), jnp.cos(ac), jnp.cos(ac)], axis=-1)
    sin = jnp.concatenate([-jnp.sin(ar), jnp.sin(ar), -jnp.sin(ac), jnp.sin(ac)], axis=-1)
    pad = LANES - rot_dim
    if pad:
        cos = jnp.concatenate([cos, jnp.ones((n_tokens, pad), F32)], axis=-1)
        sin = jnp.concatenate([sin, jnp.zeros((n_tokens, pad), F32)], axis=-1)
    return cos, sin


def _layer_weights(l, g_attn_norm, w_in, b_gate, g_mla_q, w_mla_uq, g_mla_kv, w_mla_uk, w_mla_uv,
                   g_gqa_q, g_gqa_k, w_out, g_ffn_norm, w_router, w_gate_e, w_up_e, w_down_e):
    d = D_MODEL
    o_cq, o_ckv = MLA_Q_RANK, MLA_Q_RANK + MLA_KV_RANK
    o_kr = o_ckv + MLA_ROPE
    o_gl = o_kr + (GQA_HEADS + 2 * GQA_KV_HEADS) * GQA_HEAD_DIM
    w = w_in[l]
    w_a = jnp.concatenate(
        [w[:, :o_kr], jnp.zeros((d, LANES - MLA_ROPE), F32), w[:, o_kr:o_gl]], axis=1).astype(BF16)
    w_uq = w_mla_uq[l].reshape(MLA_Q_RANK, MLA_HEADS, MLA_QK)
    w_uq = jnp.pad(w_uq, ((0, 0), (0, 0), (0, MLA_QK_PAD - MLA_QK))).reshape(MLA_Q_RANK, -1).astype(BF16)
    wr = w_router[l].T
    wr_hi = wr.astype(BF16)
    wr_lo = (wr - wr_hi.astype(F32)).astype(BF16)
    return {
        "g_attn": g_attn_norm[l][None], "g_ffn": g_ffn_norm[l][None], "w_a": w_a,
        "g_mla_q": g_mla_q[l][None], "w_uq": w_uq, "g_mla_kv": g_mla_kv[l][None],
        "g_gqa_q": g_gqa_q[l][None], "g_gqa_k": g_gqa_k[l][None],
        "w_uk": w_mla_uk[l].astype(BF16), "w_uv": w_mla_uv[l].astype(BF16),
        "w_gl": w[:, o_gl:].astype(BF16), "b_gate": b_gate[l][None], "w_out": w_out[l].astype(BF16),
        "wr_hi": wr_hi, "wr_lo": wr_lo,
        "w_gate": w_gate_e[l].astype(BF16), "w_up": w_up_e[l].astype(BF16),
        "w_down": w_down_e[l].astype(BF16),
    }


def kernel(x_prompt, x_sample, c, cache_mla_ckv, cache_mla_krope, cache_gqa_k, cache_gqa_v, c_ctx,
           w_ada, b_ada, g_attn_norm, w_in, b_gate, g_mla_q, w_mla_uq, g_mla_kv, w_mla_uk, w_mla_uv,
           g_gqa_q, g_gqa_k, w_out, g_ffn_norm, w_router, w_gate_e, w_up_e, w_down_e, g_final):
    bp, tp, d = x_prompt.shape
    bs, tl, _ = x_sample.shape
    n_layers = w_ada.shape[0]
    np_, ns = bp * tp, bs * tl
    kvw = GQA_KV_HEADS * GQA_HEAD_DIM
    cap_p = EC_FACTOR * tp // N_EXPERTS
    cap_s = EC_FACTOR * tl // N_EXPERTS
    ts = cap_s
    assert bp * cap_p == ts and bs + 1 <= 8
    n_tiles = 1 + bs
    slots = n_tiles * ts

    cc = jnp.zeros((8, d), F32).at[0].set(c_ctx).at[1:1 + bs].set(c)
    modr = _ada(cc, w_ada, b_ada).reshape(n_layers * 8 * 6, 1, d)

    cos_m, sin_m = _rope_tables(tl, MLA_ROPE)
    cos_g, sin_g = _rope_tables(tl, GQA_HEAD_DIM)
    tabs = (cos_m, sin_m, cos_g, sin_g)
    g_fin = g_final[None]

    xp = x_prompt.reshape(np_, d)
    xs = x_sample.reshape(ns, d)
    st = {"ckv": [], "kr": [], "k": [], "v": []}
    for l in range(n_layers):
        lw = _layer_weights(l, g_attn_norm, w_in, b_gate, g_mla_q, w_mla_uq, g_mla_kv, w_mla_uk,
                            w_mla_uv, g_gqa_q, g_gqa_k, w_out, g_ffn_norm, w_router, w_gate_e,
                            w_up_e, w_down_e)
        row_p = lambda r, l=l: l * 8
        row_s = lambda r, l=l: l * 8 + 1 + r // tl

        qm, ckv, kr, qg, kg, vg, ckv32, kr32, kg32, vg32 = _qkv(
            xp, modr, row_p, lw, tabs, rope=False, cache_out=True, n_lat=tl)
        st["ckv"].append(ckv32.reshape(bp, tp, MLA_KV_RANK))
        st["kr"].append(kr32.reshape(bp, tp, MLA_ROPE))
        st["k"].append(kg32.reshape(bp, tp, GQA_KV_HEADS, GQA_HEAD_DIM))
        st["v"].append(vg32.reshape(bp, tp, GQA_KV_HEADS, GQA_HEAD_DIM))
        k_m, v_m = _kv_expand(ckv.reshape(bp, tp, -1), kr.reshape(bp, tp, -1), lw["w_uk"], lw["w_uv"])
        oa_p = _attn(qm.reshape(bp, tp, -1), k_m, v_m, group=1, dq=MLA_QK_PAD)
        ob_p = _attn(qg.reshape(bp, tp, -1), kg.reshape(bp, tp, -1), vg.reshape(bp, tp, -1),
                     group=GQA_HEADS // GQA_KV_HEADS, dq=GQA_HEAD_DIM)
        x1p, aff_p = _out_proj(xp, oa_p.reshape(np_, d), ob_p.reshape(np_, d), modr, row_p, lw)

        qm, ckv, kr, qg, kg, vg = _qkv(xs, modr, row_s, lw, tabs, rope=True, cache_out=False, n_lat=tl)
        kr_cache = jnp.pad(cache_mla_krope[:, l], ((0, 0), (0, 0), (0, LANES - MLA_ROPE))).astype(BF16)
        ckv_all = jnp.concatenate([cache_mla_ckv[:, l].astype(BF16), ckv.reshape(bs, tl, -1)], axis=1)
        kr_all = jnp.concatenate([kr_cache, kr.reshape(bs, tl, -1)], axis=1)
        k_m, v_m = _kv_expand(ckv_all, kr_all, lw["w_uk"], lw["w_uv"])
        oa_s = _attn(qm.reshape(bs, tl, -1), k_m, v_m, group=1, dq=MLA_QK_PAD)
        past = cache_gqa_k.shape[2]
        k_all = jnp.concatenate([cache_gqa_k[:, l].reshape(bs, past, kvw).astype(BF16),
                                 kg.reshape(bs, tl, kvw)], axis=1)
        v_all = jnp.concatenate([cache_gqa_v[:, l].reshape(bs, past, kvw).astype(BF16),
                                 vg.reshape(bs, tl, kvw)], axis=1)
        ob_s = _attn(qg.reshape(bs, tl, -1), k_all, v_all,
                     group=GQA_HEADS // GQA_KV_HEADS, dq=GQA_HEAD_DIM)
        x1s, aff_s = _out_proj(xs, oa_s.reshape(ns, d), ob_s.reshape(ns, d), modr, row_s, lw)

        idx_p, cf_p, cw_p = _router(aff_p, bp, 0, slots)
        idx_s, cf_s, cw_s = _router(aff_s, bs, ts, slots)
        idx = jnp.concatenate([idx_p[:, :N_EXPERTS], idx_s[:, :N_EXPERTS]], axis=0).T
        idx = idx.reshape(N_EXPERTS * n_tiles, ts)
        ye = _ffn(idx, x1p, x1s, modr, l, lw, n_tiles, ts).reshape(N_EXPERTS * slots, d)
        final = l == n_layers - 1
        pad_w = lambda cw: jnp.pad(cw.T, ((0, 0), (0, LANES - N_EXPERTS)))
        tok_major = lambda cf: cf.T.reshape(-1, COMBINE_TM * N_EXPERTS)
        xp = _combine(tok_major(cf_p), ye, x1p, pad_w(cw_p), modr, row_p, g_fin, final=final)
        xs = _combine(tok_major(cf_s), ye, x1s, pad_w(cw_s), modr, row_s, g_fin, final=final)

    return (xp.reshape(bp, tp, d), xs.reshape(bs, tl, d),
            jnp.stack(st["ckv"], axis=1), jnp.stack(st["kr"], axis=1),
            jnp.stack(st["k"], axis=1), jnp.stack(st["v"], axis=1))
```

```python
import functools

import jax
import jax.numpy as jnp
from jax import lax
from jax.experimental import pallas as pl
from jax.experimental.pallas import tpu as pltpu

F32 = jnp.float32
BF16 = jnp.bfloat16

D_MODEL = 2048
GRID_W = 64
ROPE_THETA = 10000.0
EPS = 1e-6
MLA_HEADS = 16
MLA_Q_RANK = 512
MLA_KV_RANK = 256
MLA_NOPE = 128
MLA_ROPE = 64
MLA_V = 128
MLA_QK = MLA_NOPE + MLA_ROPE
MLA_SCALE = MLA_QK ** -0.5
GQA_HEADS = 16
GQA_KV_HEADS = 4
GQA_GROUP = GQA_HEADS // GQA_KV_HEADS
GQA_HEAD_DIM = 128
GQA_SCALE = GQA_HEAD_DIM ** -0.5
HEAD_V = 128
N_EXPERTS = 16
EC_FACTOR = 2
D_FF = 1024
LOG2E = 1.4426950408889634

LANES = 128
SUBLANES = 8
MXU_DIM = 256
MLA_QK_PAD = MXU_DIM
VMEM_LIMIT = 56 * 1024 * 1024

ROW_TM = 256
ATTN_TQ = 512
ATTN_TK = 512
COMBINE_TM = 128
FFN_GATE_CHUNKS = 4
FFN_DOWN_CHUNKS = 4

C_CQ = 0
C_CKV = C_CQ + MLA_Q_RANK
C_KR = C_CKV + MLA_KV_RANK
C_GQ = C_KR + LANES
C_GK = C_GQ + GQA_HEADS * GQA_HEAD_DIM
C_GV = C_GK + GQA_KV_HEADS * GQA_HEAD_DIM
C_END = C_GV + GQA_KV_HEADS * GQA_HEAD_DIM


def _dot(a, b):
    return jnp.dot(a, b, preferred_element_type=F32)


def _dot_nt(a, b):
    return lax.dot_general(a, b, (((1,), (1,)), ((), ())), preferred_element_type=F32)


def _split(x):
    hi = x.astype(BF16)
    lo = (x - hi.astype(F32)).astype(BF16)
    return hi, lo


def _rms(x):
    return x * lax.rsqrt(jnp.mean(x * x, axis=-1, keepdims=True) + EPS)


def _params(sem):
    return pltpu.CompilerParams(dimension_semantics=sem, vmem_limit_bytes=VMEM_LIMIT)


def _const_spec(shape):
    nd = len(shape)
    return pl.BlockSpec(shape, lambda *_: (0,) * nd, pipeline_mode=pl.Buffered(1))


def _ada_kernel(c_ref, w_ref, b_ref, o_ref):
    c = c_ref[...]
    s = c / (1.0 + jnp.exp(-c))
    s_hi, s_lo = _split(s)
    w_hi, w_lo = _split(w_ref[...])
    o_ref[...] = _dot(s_hi, w_hi) + _dot(s_hi, w_lo) + _dot(s_lo, w_hi) + b_ref[...]


def _ada(cc, w_ada, b_ada):
    n_layers, d, n6 = w_ada.shape
    rows = cc.shape[0]
    tn = 1024
    return pl.pallas_call(
        _ada_kernel,
        grid=(n_layers, n6 // tn),
        in_specs=[
            pl.BlockSpec((rows, d), lambda l, j: (0, 0)),
            pl.BlockSpec((None, d, tn), lambda l, j: (l, 0, j)),
            pl.BlockSpec((None, 1, tn), lambda l, j: (l, 0, j)),
        ],
        out_specs=pl.BlockSpec((None, rows, tn), lambda l, j: (l, 0, j)),
        out_shape=jax.ShapeDtypeStruct((n_layers, rows, n6), F32),
        compiler_params=_params(("arbitrary", "arbitrary")),
        name="ada",
    )(cc, w_ada, b_ada.reshape(n_layers, 1, n6))


def _swap_halves(x, q):
    lane = lax.broadcasted_iota(jnp.int32, x.shape, 1)
    first = (lane % (2 * q)) < q
    return jnp.where(first, pltpu.roll(x, LANES - q, 1), pltpu.roll(x, q, 1))


def _qkv_kernel(rope, cache_out, x_ref, sh_ref, sc_ref, gn_ref, w_ref, gq_ref, wuq_ref, gkv_ref,
                ggq_ref, ggk_ref, cm_ref, sm_ref, cg_ref, sg_ref, *outs):
    qm_ref, ckv_ref, kr_ref, qg_ref, kg_ref, vg_ref = outs[:6]
    x = x_ref[...]
    h = (_rms(x) * gn_ref[...] * (1.0 + sc_ref[...]) + sh_ref[...]).astype(BF16)

    cq = _rms(_dot(h, w_ref[:, C_CQ:C_CKV])) * gq_ref[...]
    qm = _dot(cq.astype(BF16), wuq_ref[...]) * (MLA_SCALE * LOG2E)
    for hd in range(MLA_HEADS):
        lo = hd * MLA_QK_PAD
        qm_ref[hd, :, :LANES] = qm[:, lo:lo + LANES].astype(BF16)
        t = qm[:, lo + LANES:lo + 2 * LANES]
        if rope:
            t = t * cm_ref[...] + _swap_halves(t, MLA_ROPE // 4) * sm_ref[...]
        qm_ref[hd, :, LANES:] = t.astype(BF16)

    ckv = _rms(_dot(h, w_ref[:, C_CKV:C_KR])) * gkv_ref[...]
    ckv_ref[...] = ckv.astype(BF16)
    kr = _dot(h, w_ref[:, C_KR:C_GQ])
    if cache_out:
        outs[6][...] = ckv
        outs[7][...] = kr[:, :MLA_ROPE]
    if rope:
        kr = kr * cm_ref[...] + _swap_halves(kr, MLA_ROPE // 4) * sm_ref[...]
    kr_ref[...] = kr.astype(BF16)

    zq = _dot(h, w_ref[:, C_GQ:C_GK])
    for hd in range(GQA_HEADS):
        lo = hd * GQA_HEAD_DIM
        t = _rms(zq[:, lo:lo + GQA_HEAD_DIM]) * ggq_ref[...]
        if rope:
            t = t * cg_ref[...] + _swap_halves(t, GQA_HEAD_DIM // 4) * sg_ref[...]
        qg_ref[hd] = (t * (GQA_SCALE * LOG2E)).astype(BF16)
    zk = _dot(h, w_ref[:, C_GK:C_GV])
    for hd in range(GQA_KV_HEADS):
        lo = hd * GQA_HEAD_DIM
        t = _rms(zk[:, lo:lo + GQA_HEAD_DIM]) * ggk_ref[...]
        if cache_out:
            outs[8][:, lo:lo + GQA_HEAD_DIM] = t
        if rope:
            t = t * cg_ref[...] + _swap_halves(t, GQA_HEAD_DIM // 4) * sg_ref[...]
        kg_ref[:, lo:lo + GQA_HEAD_DIM] = t.astype(BF16)
    zv = _dot(h, w_ref[:, C_GV:C_END])
    vg_ref[...] = zv.astype(BF16)
    if cache_out:
        outs[9][...] = zv


def _qkv(x, t_len, modr, mod_row, lw, tabs, *, rope, cache_out):
    n, d = x.shape
    b = n // t_len
    tm = min(ROW_TM, t_len)
    tpb = t_len // tm
    pos_tiles = tabs[0].shape[0] // tm
    kvw = GQA_KV_HEADS * GQA_HEAD_DIM
    row = lambda w: pl.BlockSpec((tm, w), lambda i: (i, 0))
    row_in = row(d)
    heads = lambda nh, w: pl.BlockSpec((None, nh, tm, w), lambda i: (i // tpb, 0, i % tpb, 0))
    mod = lambda j: pl.BlockSpec((None, 1, d), lambda i: (mod_row(i * tm) * 6 + j, 0, 0))
    tab = lambda w: pl.BlockSpec((tm, w), lambda i: (i % pos_tiles, 0))
    in_specs = [
        row_in, mod(0), mod(1), _const_spec((1, d)), _const_spec((d, C_END)),
        _const_spec((1, MLA_Q_RANK)), _const_spec((MLA_Q_RANK, MLA_HEADS * MLA_QK_PAD)),
        _const_spec((1, MLA_KV_RANK)), _const_spec((1, GQA_HEAD_DIM)), _const_spec((1, GQA_HEAD_DIM)),
        tab(LANES), tab(LANES), tab(LANES), tab(LANES),
    ]
    out_shape = [
        jax.ShapeDtypeStruct((b, MLA_HEADS, t_len, MLA_QK_PAD), BF16),
        jax.ShapeDtypeStruct((n, MLA_KV_RANK), BF16),
        jax.ShapeDtypeStruct((n, LANES), BF16),
        jax.ShapeDtypeStruct((b, GQA_HEADS, t_len, GQA_HEAD_DIM), BF16),
        jax.ShapeDtypeStruct((n, kvw), BF16),
        jax.ShapeDtypeStruct((n, kvw), BF16),
    ]
    out_specs = [heads(MLA_HEADS, MLA_QK_PAD), row(MLA_KV_RANK), row(LANES),
                 heads(GQA_HEADS, GQA_HEAD_DIM), row(kvw), row(kvw)]
    if cache_out:
        out_shape += [
            jax.ShapeDtypeStruct((n, MLA_KV_RANK), F32),
            jax.ShapeDtypeStruct((n, MLA_ROPE), F32),
            jax.ShapeDtypeStruct((n, kvw), F32),
            jax.ShapeDtypeStruct((n, kvw), F32),
        ]
        out_specs += [row(MLA_KV_RANK), row(MLA_ROPE), row(kvw), row(kvw)]
    return pl.pallas_call(
        functools.partial(_qkv_kernel, rope, cache_out),
        grid=(n // tm,),
        in_specs=in_specs,
        out_specs=out_specs,
        out_shape=out_shape,
        compiler_params=_params(("arbitrary",)),
        name="qkv_rope" if rope else "qkv_ctx",
    )(x, modr, modr, lw["g_attn"], lw["w_a"], lw["g_mla_q"], lw["w_uq"], lw["g_mla_kv"],
      lw["g_gqa_q"], lw["g_gqa_k"], *tabs)


def _kv_expand_kernel(ckv_ref, kr_ref, wuk_ref, wuv_ref, k_ref, v_ref):
    ckv = ckv_ref[...]
    kn = _dot(ckv, wuk_ref[...]).astype(BF16)
    kr = kr_ref[...]
    for hd in range(MLA_HEADS):
        k_ref[:, hd * MLA_QK_PAD:hd * MLA_QK_PAD + LANES] = kn[:, hd * MLA_NOPE:(hd + 1) * MLA_NOPE]
        k_ref[:, hd * MLA_QK_PAD + LANES:(hd + 1) * MLA_QK_PAD] = kr
    v_ref[...] = _dot(ckv, wuv_ref[...]).astype(BF16)


def _kv_expand(ckv, kr, w_uk, w_uv):
    b, s, _ = ckv.shape
    ts = min(512, s)
    kw, vw = MLA_HEADS * MLA_QK_PAD, MLA_HEADS * MLA_V
    blk = lambda w: pl.BlockSpec((None, ts, w), lambda i, j: (i, j, 0))
    return pl.pallas_call(
        _kv_expand_kernel,
        grid=(b, s // ts),
        in_specs=[blk(MLA_KV_RANK), blk(LANES), _const_spec((MLA_KV_RANK, MLA_HEADS * MLA_NOPE)),
                  _const_spec((MLA_KV_RANK, vw))],
        out_specs=[blk(kw), blk(vw)],
        out_shape=[jax.ShapeDtypeStruct((b, s, kw), BF16), jax.ShapeDtypeStruct((b, s, vw), BF16)],
        compiler_params=_params(("arbitrary", "arbitrary")),
        name="kv_expand",
    )(ckv, kr, w_uk, w_uv)


def _attn_kernel(tq, tk, dq, hps, rolled, q_ref, k_ref, v_ref, o_ref, s0, s1, m0, m1, vx_ref):
    s_len = k_ref.shape[0]
    rows = q_ref.shape[1]
    n_chunks = s_len // tk
    dv = HEAD_V
    bufs = ((s0, m0), (s1, m1))

    for hh in range(hps):
        vx_ref[hh, :, :dv] = v_ref[:, hh * dv:(hh + 1) * dv]
        vx_ref[hh, :, dv:] = jnp.ones((s_len, dv), BF16)

    def stage(p1, p2):
        if p1 is not None:
            h1, r1, b1 = p1
            q = q_ref[h1, pl.ds(r1, tq), :]
            sb1, mb1 = bufs[b1]
        if p2 is not None:
            h2, r2, b2 = p2
            sb2, mb2 = bufs[b2]
            m = jnp.max(mb2[...], axis=1, keepdims=True)
            acc = None
        for c in range(n_chunks):
            ck = slice(c * tk, (c + 1) * tk)
            if p1 is not None:
                s = _dot_nt(q, k_ref[ck, h1 * dq:(h1 + 1) * dq])
                sb1[:, ck] = s
                mx = s[:, :LANES]
                for t in range(1, tk // LANES):
                    mx = jnp.maximum(mx, s[:, t * LANES:(t + 1) * LANES])
                mb1[...] = mx if c == 0 else jnp.maximum(mb1[...], mx)
            if p2 is not None:
                p = jnp.exp2(sb2[:, ck] - m).astype(BF16)
                part = _dot(p, vx_ref[h2, ck, :])
                acc = part if acc is None else acc + part
        if p2 is not None:
            o_ref[h2, pl.ds(r2, tq), :] = (acc[:, :dv] / acc[:, dv:]).astype(BF16)

    n_units = rows // tq
    if rolled:
        stage((0, 0, 0), None)

        def body(i, carry):
            u = 2 * i
            stage((0, pl.multiple_of((u + 1) * tq, tq), 1), (0, pl.multiple_of(u * tq, tq), 0))
            nxt = jnp.minimum(u + 2, n_units - 1)
            stage((0, pl.multiple_of(nxt * tq, tq), 0), (0, pl.multiple_of((u + 1) * tq, tq), 1))
            return carry

        lax.fori_loop(0, n_units // 2, body, 0)
    else:
        units = [(hh, u * tq) for hh in range(hps) for u in range(n_units)]
        for j in range(len(units) + 1):
            p1 = units[j] + (j % 2,) if j < len(units) else None
            p2 = units[j - 1] + ((j - 1) % 2,) if j > 0 else None
            stage(p1, p2)


def _attn(q, k, v, *, hps, rolled):
    b, hk, rows, dq = q.shape
    s = k.shape[1]
    dv = HEAD_V
    tq = min(ATTN_TQ, rows)
    tk = min(ATTN_TK, s)
    assert rows % tq == 0 and s % tk == 0 and hk % hps == 0
    assert not rolled or (hps == 1 and (rows // tq) % 2 == 0)
    return pl.pallas_call(
        functools.partial(_attn_kernel, tq, tk, dq, hps, rolled),
        grid=(b, hk // hps),
        in_specs=[
            pl.BlockSpec((None, hps, rows, dq), lambda i, h: (i, h, 0, 0)),
            pl.BlockSpec((None, s, hps * dq), lambda i, h: (i, 0, h)),
            pl.BlockSpec((None, s, hps * dv), lambda i, h: (i, 0, h)),
        ],
        out_specs=pl.BlockSpec((None, hps, rows, dv), lambda i, h: (i, h, 0, 0)),
        out_shape=jax.ShapeDtypeStruct((b, hk, rows, dv), BF16),
        scratch_shapes=[pltpu.VMEM((tq, s), F32), pltpu.VMEM((tq, s), F32),
                        pltpu.VMEM((tq, LANES), F32), pltpu.VMEM((tq, LANES), F32),
                        pltpu.VMEM((hps, s, 2 * dv), BF16)],
        compiler_params=_params(("arbitrary", "arbitrary")),
        name="attn_rolled" if rolled else "attn_flat",
    )(q, k, v)


def _out_proj_kernel(x_ref, oa_ref, ob_ref, sh1_ref, sc1_ref, gt1_ref, sh2_ref, sc2_ref, gn1_ref,
                     gn2_ref, wgl_ref, bg_ref, wo_ref, wrh_ref, wrl_ref, x1_ref, aff_ref):
    d = x_ref.shape[1]
    x = x_ref[...]
    h = (_rms(x) * gn1_ref[...] * (1.0 + sc1_ref[...]) + sh1_ref[...]).astype(BF16)
    ga = 1.0 / (1.0 + jnp.exp(-(_dot(h, wgl_ref[:, :d]) + bg_ref[:, :d])))
    gb = 1.0 / (1.0 + jnp.exp(-(_dot(h, wgl_ref[:, d:]) + bg_ref[:, d:])))
    parts = []
    for hd in range(d // HEAD_V):
        cs = slice(hd * HEAD_V, (hd + 1) * HEAD_V)
        parts.append((ga[:, cs] * oa_ref[hd].astype(F32) + gb[:, cs] * ob_ref[hd].astype(F32)).astype(BF16))
    merged = jnp.concatenate(parts, axis=1)
    x1 = x + gt1_ref[...] * _dot(merged, wo_ref[...])
    x1_ref[...] = x1
    h2 = _rms(x1) * gn2_ref[...] * (1.0 + sc2_ref[...]) + sh2_ref[...]
    h_hi, h_lo = _split(h2)
    lg = _dot_nt(wrh_ref[...], h_hi) + _dot_nt(wrh_ref[...], h_lo) + _dot_nt(wrl_ref[...], h_hi)
    e = jnp.exp(lg - jnp.max(lg, axis=0, keepdims=True))
    aff_ref[...] = e / jnp.sum(e, axis=0, keepdims=True)


def _out_proj(x, t_len, oa, ob, modr, mod_row, lw):
    n, d = x.shape
    tm = min(ROW_TM, t_len)
    tpb = t_len // tm
    nh = d // HEAD_V
    row = lambda: pl.BlockSpec((tm, d), lambda i: (i, 0))
    heads = lambda: pl.BlockSpec((None, nh, tm, HEAD_V), lambda i: (i // tpb, 0, i % tpb, 0))
    mod = lambda j: pl.BlockSpec((None, 1, d), lambda i: (mod_row(i * tm) * 6 + j, 0, 0))
    return pl.pallas_call(
        _out_proj_kernel,
        grid=(n // tm,),
        in_specs=[row(), heads(), heads(), mod(0), mod(1), mod(2), mod(3), mod(4),
                  _const_spec((1, d)), _const_spec((1, d)), _const_spec((d, 2 * d)),
                  _const_spec((1, 2 * d)), _const_spec((d, d)),
                  _const_spec((N_EXPERTS, d)), _const_spec((N_EXPERTS, d))],
        out_specs=[row(), pl.BlockSpec((N_EXPERTS, tm), lambda i: (0, i))],
        out_shape=[jax.ShapeDtypeStruct((n, d), F32), jax.ShapeDtypeStruct((N_EXPERTS, n), F32)],
        compiler_params=_params(("arbitrary",)),
        name="out_proj",
    )(x, oa, ob, modr, modr, modr, modr, modr, lw["g_attn"], lw["g_ffn"], lw["w_gl"], lw["b_gate"],
      lw["w_out"], lw["wr_hi"], lw["wr_lo"])


def _lane_cumsum(x_bf16, tri):
    t = x_bf16.shape[1]
    blk = tri.shape[0]
    parts = []
    carry = jnp.zeros((x_bf16.shape[0], 1), F32)
    for c in range(t // blk):
        part = _dot(x_bf16[:, c * blk:(c + 1) * blk], tri) + carry
        carry = part[:, blk - 1:blk]
        parts.append(part)
    return parts[0] if len(parts) == 1 else jnp.concatenate(parts, axis=1)


def _router_kernel(cap, tok_base, aff_ref, idx_ref, w_ref, pos_ref):
    r = pl.program_id(0)
    a = aff_ref[...]
    n_e, t = a.shape
    bits = pltpu.bitcast(a, jnp.int32)

    thr = jnp.zeros((n_e, 1), jnp.int32)
    for b in range(30, -1, -1):
        cand = thr | (1 << b)
        cnt = jnp.sum((bits >= cand).astype(jnp.int32), axis=1, keepdims=True)
        thr = jnp.where(cnt >= cap, cand, thr)
    gt = bits > thr
    eq = bits == thr
    need = (cap - jnp.sum(gt.astype(jnp.int32), axis=1, keepdims=True)).astype(F32)

    blk = min(MXU_DIM, t)
    ri = lax.broadcasted_iota(jnp.int32, (blk, blk), 0)
    ci = lax.broadcasted_iota(jnp.int32, (blk, blk), 1)
    tri = (ri <= ci).astype(BF16)
    cum_eq = _lane_cumsum(eq.astype(BF16), tri)
    sel = gt | (eq & (cum_eq <= need))
    cnt_incl = _lane_cumsum(sel.astype(BF16), tri)
    w_ref[...] = jnp.where(sel, a, 0.0)
    pos_ref[...] = (cnt_incl - sel.astype(F32)).astype(jnp.int32) + r * cap

    ts = min(LANES, cap)
    ones = jnp.ones((t, LANES), BF16)
    lane = lax.broadcasted_iota(jnp.int32, (1, LANES), 1)
    for sb in range(cap // ts):
        s_iota = (lax.broadcasted_iota(jnp.int32, (ts, 1), 0) + sb * ts).astype(F32)
        out = jnp.zeros((ts, LANES), F32)
        for e in range(n_e):
            cmp = (cnt_incl[e:e + 1, :] <= s_iota).astype(BF16)
            out = out + jnp.where(lane == e, _dot(cmp, ones), 0.0)
        idx_ref[sb * ts:(sb + 1) * ts, :] = out.astype(jnp.int32) + (r * t + tok_base)


def _router(aff_t, n_req, tok_base):
    n_e, n = aff_t.shape
    t = n // n_req
    cap = EC_FACTOR * t // N_EXPERTS
    return pl.pallas_call(
        functools.partial(_router_kernel, cap, tok_base),
        grid=(n_req,),
        in_specs=[pl.BlockSpec((n_e, t), lambda r: (0, r))],
        out_specs=[pl.BlockSpec((cap, LANES), lambda r: (r, 0)),
                   pl.BlockSpec((n_e, t), lambda r: (0, r)),
                   pl.BlockSpec((n_e, t), lambda r: (0, r))],
        out_shape=[jax.ShapeDtypeStruct((n_req * cap, LANES), jnp.int32),
                   jax.ShapeDtypeStruct((n_e, n), F32),
                   jax.ShapeDtypeStruct((n_e, n), jnp.int32)],
        compiler_params=_params(("arbitrary",)),
        name="router",
    )(aff_t)


def _ffn_kernel(ts, idx_ref, x_hbm, sh_ref, sc_ref, gn_ref, wg_ref, wu_ref, wd_ref, ye_ref, xbuf, sem):
    n_tiles = pl.num_programs(1)
    step = pl.program_id(0) * n_tiles + pl.program_id(1)
    last = pl.num_programs(0) * n_tiles - 1
    slot = step % 2
    nslot = 1 - slot
    d = xbuf.shape[2]

    def row_copy(tile, s, buf):
        return pltpu.make_async_copy(x_hbm.at[pl.ds(idx_ref[tile * ts + s], 1), :],
                                     xbuf.at[buf, pl.ds(s, 1), :], sem.at[buf])

    def wait_tile(buf):
        pltpu.make_async_copy(x_hbm.at[pl.ds(0, ts), :], xbuf.at[buf], sem.at[buf]).wait()

    @pl.when(step == 0)
    def _():
        def body(s, carry):
            row_copy(0, s, 0).start()
            return carry
        lax.fori_loop(0, ts, body, 0, unroll=8)

    wait_tile(slot)
    nxt = jnp.minimum(step + 1, last)
    n_groups = FFN_GATE_CHUNKS + FFN_DOWN_CHUNKS
    per_group = ts // n_groups
    issued = [0]

    def issue_group():
        for s in range(issued[0], issued[0] + per_group):
            row_copy(nxt, s, nslot).start()
        issued[0] += per_group

    h = (_rms(xbuf[slot]) * gn_ref[...] * (1.0 + sc_ref[...]) + sh_ref[...]).astype(BF16)
    fw = D_FF // FFN_GATE_CHUNKS
    hid = []
    for c in range(FFN_GATE_CHUNKS):
        gate = _dot(h, wg_ref[:, c * fw:(c + 1) * fw])
        up = _dot(h, wu_ref[:, c * fw:(c + 1) * fw])
        hid.append((gate / (1.0 + jnp.exp(-gate)) * up).astype(BF16))
        issue_group()
    hid = jnp.concatenate(hid, axis=1)
    dw = d // FFN_DOWN_CHUNKS
    for c in range(FFN_DOWN_CHUNKS):
        ye_ref[:, c * dw:(c + 1) * dw] = _dot(hid, wd_ref[:, c * dw:(c + 1) * dw])
        issue_group()

    @pl.when(step == last)
    def _():
        wait_tile(nslot)


def _ffn(idx, x1, modr, layer, lw, n_tiles, ts):
    d = x1.shape[1]
    assert ts % (FFN_GATE_CHUNKS + FFN_DOWN_CHUNKS) == 0
    assert D_FF % FFN_GATE_CHUNKS == 0 and d % FFN_DOWN_CHUNKS == 0
    mod = lambda c: pl.BlockSpec((None, 1, d), lambda e, j, *_: ((layer * 8 + j) * 6 + c, 0, 0))
    grid_spec = pltpu.PrefetchScalarGridSpec(
        num_scalar_prefetch=1,
        grid=(N_EXPERTS, n_tiles),
        in_specs=[
            pl.BlockSpec(memory_space=pl.ANY), mod(3), mod(4),
            pl.BlockSpec((1, d), lambda e, j, *_: (0, 0)),
            pl.BlockSpec((None, d, D_FF), lambda e, j, *_: (e, 0, 0)),
            pl.BlockSpec((None, d, D_FF), lambda e, j, *_: (e, 0, 0)),
            pl.BlockSpec((None, D_FF, d), lambda e, j, *_: (e, 0, 0)),
        ],
        out_specs=pl.BlockSpec((None, ts, d), lambda e, j, *_: (e, j, 0)),
        scratch_shapes=[pltpu.VMEM((2, ts, d), F32), pltpu.SemaphoreType.DMA((2,))],
    )
    return pl.pallas_call(
        functools.partial(_ffn_kernel, ts),
        grid_spec=grid_spec,
        out_shape=jax.ShapeDtypeStruct((N_EXPERTS, n_tiles * ts, d), F32),
        compiler_params=_params(("arbitrary", "arbitrary")),
        name="ffn",
    )(idx, x1, modr, modr, lw["g_ffn"], lw["w_gate"], lw["w_up"], lw["w_down"])


def _combine_kernel(final, tm, gslots, slot_stride, group_base, row_off, starts_ref, idx_ref, ye_hbm,
                    x_ref, w_ref, gt_ref, gf_ref, o_ref, rows, sem):
    i = pl.program_id(0)
    n_e = N_EXPERTS

    total = 0
    for e in range(n_e):
        s_lo = starts_ref[i * n_e + e]
        s_hi = starts_ref[(i + 1) * n_e + e]

        def body(s, carry, e=e):
            tok = idx_ref[e * gslots + s] - (row_off + i * tm)
            pltpu.make_async_copy(ye_hbm.at[pl.ds(e * slot_stride + group_base + s, 1), :],
                                  rows.at[pl.ds(e * tm + tok, 1), :], sem).start()
            return carry

        lax.fori_loop(s_lo, s_hi, body, 0)
        total = total + (s_hi - s_lo)

    n8 = pl.multiple_of((total // SUBLANES) * SUBLANES, SUBLANES)

    @pl.when(n8 > 0)
    def _():
        pltpu.make_async_copy(ye_hbm.at[pl.ds(0, n8), :], rows.at[pl.ds(0, n8), :], sem).wait()

    def wait_row(_, carry):
        pltpu.make_async_copy(ye_hbm.at[pl.ds(0, 1), :], rows.at[pl.ds(0, 1), :], sem).wait()
        return carry

    lax.fori_loop(0, total - n8, wait_row, 0)

    w = w_ref[...]
    acc = jnp.zeros(x_ref.shape, F32)
    for e in range(n_e):
        we = w[:, e:e + 1]
        acc = acc + jnp.where(we > 0.0, we * rows[e * tm:(e + 1) * tm, :], 0.0)
    y = x_ref[...] + gt_ref[...] * acc
    if final:
        y = _rms(y) * gf_ref[...]
    o_ref[...] = y


def _combine(starts, idx_g, ye, x1, row_off, n, w, modr, mod_row, g_final, *, slot_stride, group_base,
             final):
    d = x1.shape[1]
    tm = COMBINE_TM
    gslots = idx_g.shape[0] // N_EXPERTS
    off = row_off // tm
    grid_spec = pltpu.PrefetchScalarGridSpec(
        num_scalar_prefetch=2,
        grid=(n // tm,),
        in_specs=[
            pl.BlockSpec(memory_space=pl.ANY),
            pl.BlockSpec((tm, d), lambda i, *_: (i + off, 0)),
            pl.BlockSpec((tm, LANES), lambda i, *_: (i, 0)),
            pl.BlockSpec((None, 1, d), lambda i, *_: (mod_row(i * tm) * 6 + 5, 0, 0)),
            pl.BlockSpec((1, d), lambda i, *_: (0, 0)),
        ],
        out_specs=pl.BlockSpec((tm, d), lambda i, *_: (i, 0)),
        scratch_shapes=[pltpu.VMEM((N_EXPERTS * tm, d), F32), pltpu.SemaphoreType.DMA(())],
    )
    return pl.pallas_call(
        functools.partial(_combine_kernel, final, tm, gslots, slot_stride, group_base, row_off),
        grid_spec=grid_spec,
        out_shape=jax.ShapeDtypeStruct((n, d), F32),
        compiler_params=_params(("arbitrary",)),
        name="combine_final" if final else "combine",
    )(starts, idx_g, ye, x1, w, modr, g_final)


def _rope_tables(n_tokens, rot_dim):
    t = jnp.arange(n_tokens)
    row = (t // GRID_W).astype(F32)
    col = (t % GRID_W).astype(F32)
    quarter = rot_dim // 4
    inv = ROPE_THETA ** (-jnp.arange(quarter, dtype=F32) / quarter)
    ar, ac = row[:, None] * inv, col[:, None] * inv
    cos = jnp.concatenate([jnp.cos(ar), jnp.cos(ar), jnp.cos(ac), jnp.cos(ac)], axis=-1)
    sin = jnp.concatenate([-jnp.sin(ar), jnp.sin(ar), -jnp.sin(ac), jnp.sin(ac)], axis=-1)
    pad = LANES - rot_dim
    if pad:
        cos = jnp.concatenate([cos, jnp.ones((n_tokens, pad), F32)], axis=-1)
        sin = jnp.concatenate([sin, jnp.zeros((n_tokens, pad), F32)], axis=-1)
    return cos, sin


def _layer_weights(l, g_attn_norm, w_in, b_gate, g_mla_q, w_mla_uq, g_mla_kv, w_mla_uk, w_mla_uv,
                   g_gqa_q, g_gqa_k, w_out, g_ffn_norm, w_router, w_gate_e, w_up_e, w_down_e):
    d = D_MODEL
    o_ckv = MLA_Q_RANK + MLA_KV_RANK
    o_kr = o_ckv + MLA_ROPE
    o_gl = o_kr + (GQA_HEADS + 2 * GQA_KV_HEADS) * GQA_HEAD_DIM
    w = w_in[l]
    w_a = jnp.concatenate(
        [w[:, :o_kr], jnp.zeros((d, LANES - MLA_ROPE), F32), w[:, o_kr:o_gl]], axis=1).astype(BF16)
    w_uq = w_mla_uq[l].reshape(MLA_Q_RANK, MLA_HEADS, MLA_QK)
    w_uq = jnp.pad(w_uq, ((0, 0), (0, 0), (0, MLA_QK_PAD - MLA_QK))).reshape(MLA_Q_RANK, -1).astype(BF16)
    wr = w_router[l].T
    wr_hi = wr.astype(BF16)
    wr_lo = (wr - wr_hi.astype(F32)).astype(BF16)
    return {
        "g_attn": g_attn_norm[l][None], "g_ffn": g_ffn_norm[l][None], "w_a": w_a,
        "g_mla_q": g_mla_q[l][None], "w_uq": w_uq, "g_mla_kv": g_mla_kv[l][None],
        "g_gqa_q": g_gqa_q[l][None], "g_gqa_k": g_gqa_k[l][None],
        "w_uk": w_mla_uk[l].astype(BF16), "w_uv": w_mla_uv[l].astype(BF16),
        "w_gl": w[:, o_gl:].astype(BF16), "b_gate": b_gate[l][None], "w_out": w_out[l].astype(BF16),
        "wr_hi": wr_hi, "wr_lo": wr_lo,
        "w_gate": w_gate_e[l].astype(BF16), "w_up": w_up_e[l].astype(BF16),
        "w_down": w_down_e[l].astype(BF16),
    }


def _combine_tables(idx, pos, n_slots):
    starts = jnp.concatenate([pos[:, ::COMBINE_TM].T,
                              jnp.full((1, N_EXPERTS), n_slots, jnp.int32)], axis=0)
    return starts.reshape(-1), idx[:, :N_EXPERTS].T.reshape(-1)


def kernel(x_prompt, x_sample, c, cache_mla_ckv, cache_mla_krope, cache_gqa_k, cache_gqa_v, c_ctx,
           w_ada, b_ada, g_attn_norm, w_in, b_gate, g_mla_q, w_mla_uq, g_mla_kv, w_mla_uk, w_mla_uv,
           g_gqa_q, g_gqa_k, w_out, g_ffn_norm, w_router, w_gate_e, w_up_e, w_down_e, g_final):
    bp, tp, d = x_prompt.shape
    bs, tl, _ = x_sample.shape
    n_layers = w_ada.shape[0]
    np_, ns = bp * tp, bs * tl
    kvw = GQA_KV_HEADS * GQA_HEAD_DIM
    cap_p = EC_FACTOR * tp // N_EXPERTS
    cap_s = EC_FACTOR * tl // N_EXPERTS
    ts = cap_s
    assert bp * cap_p == ts and bs + 1 <= 8
    n_tiles = 1 + bs
    slots = n_tiles * ts

    cc = jnp.zeros((8, d), F32).at[0].set(c_ctx).at[1:1 + bs].set(c)
    modr = _ada(cc, w_ada, b_ada).reshape(n_layers * 8 * 6, 1, d)

    cos_m, sin_m = _rope_tables(tl, MLA_ROPE)
    cos_g, sin_g = _rope_tables(tl, GQA_HEAD_DIM)
    tabs = (cos_m, sin_m, cos_g, sin_g)
    g_fin = g_final[None]
    pad_w = lambda w: jnp.pad(w.T, ((0, 0), (0, LANES - N_EXPERTS)))

    xp = x_prompt.reshape(np_, d)
    xs = x_sample.reshape(ns, d)
    st = {"ckv": [], "kr": [], "k": [], "v": []}
    for l in range(n_layers):
        lw = _layer_weights(l, g_attn_norm, w_in, b_gate, g_mla_q, w_mla_uq, g_mla_kv, w_mla_uk,
                            w_mla_uv, g_gqa_q, g_gqa_k, w_out, g_ffn_norm, w_router, w_gate_e,
                            w_up_e, w_down_e)
        row_p = lambda r, l=l: l * 8
        row_s = lambda r, l=l: l * 8 + 1 + r // tl

        qm, ckv, kr, qg, kg, vg, ckv32, kr32, kg32, vg32 = _qkv(
            xp, tp, modr, row_p, lw, tabs, rope=False, cache_out=True)
        st["ckv"].append(ckv32.reshape(bp, tp, MLA_KV_RANK))
        st["kr"].append(kr32.reshape(bp, tp, MLA_ROPE))
        st["k"].append(kg32.reshape(bp, tp, GQA_KV_HEADS, GQA_HEAD_DIM))
        st["v"].append(vg32.reshape(bp, tp, GQA_KV_HEADS, GQA_HEAD_DIM))
        k_m, v_m = _kv_expand(ckv.reshape(bp, tp, -1), kr.reshape(bp, tp, -1), lw["w_uk"], lw["w_uv"])
        oa_p = _attn(qm, k_m, v_m, hps=MLA_HEADS, rolled=False)
        ob_p = _attn(qg.reshape(bp, GQA_KV_HEADS, GQA_GROUP * tp, GQA_HEAD_DIM), kg.reshape(bp, tp, -1),
                     vg.reshape(bp, tp, -1), hps=GQA_KV_HEADS, rolled=False)
        x1p, aff_p = _out_proj(xp, tp, oa_p, ob_p.reshape(bp, GQA_HEADS, tp, HEAD_V), modr, row_p, lw)

        qm, ckv, kr, qg, kg, vg = _qkv(xs, tl, modr, row_s, lw, tabs, rope=True, cache_out=False)
        kr_cache = jnp.pad(cache_mla_krope[:, l], ((0, 0), (0, 0), (0, LANES - MLA_ROPE))).astype(BF16)
        ckv_all = jnp.concatenate([cache_mla_ckv[:, l].astype(BF16), ckv.reshape(bs, tl, -1)], axis=1)
        kr_all = jnp.concatenate([kr_cache, kr.reshape(bs, tl, -1)], axis=1)
        k_m, v_m = _kv_expand(ckv_all, kr_all, lw["w_uk"], lw["w_uv"])
        oa_s = _attn(qm, k_m, v_m, hps=1, rolled=True)
        past = cache_gqa_k.shape[2]
        k_all = jnp.concatenate([cache_gqa_k[:, l].reshape(bs, past, kvw).astype(BF16),
                                 kg.reshape(bs, tl, kvw)], axis=1)
        v_all = jnp.concatenate([cache_gqa_v[:, l].reshape(bs, past, kvw).astype(BF16),
                                 vg.reshape(bs, tl, kvw)], axis=1)
        ob_s = _attn(qg.reshape(bs, GQA_KV_HEADS, GQA_GROUP * tl, GQA_HEAD_DIM), k_all, v_all,
                     hps=1, rolled=True)
        x1s, aff_s = _out_proj(xs, tl, oa_s, ob_s.reshape(bs, GQA_HEADS, tl, HEAD_V), modr, row_s, lw)

        x1 = jnp.concatenate([x1p, x1s], axis=0)
        idx_p, w_p, pos_p = _router(aff_p, bp, 0)
        idx_s, w_s, pos_s = _router(aff_s, bs, np_)
        idx = jnp.concatenate([idx_p[:, :N_EXPERTS], idx_s[:, :N_EXPERTS]], axis=0).T.reshape(-1)
        ye = _ffn(idx, x1, modr, l, lw, n_tiles, ts).reshape(N_EXPERTS * slots, d)
        final = l == n_layers - 1
        st_p, ig_p = _combine_tables(idx_p, pos_p, bp * cap_p)
        st_s, ig_s = _combine_tables(idx_s, pos_s, bs * cap_s)
        xp = _combine(st_p, ig_p, ye, x1, 0, np_, pad_w(w_p), modr, row_p, g_fin,
                      slot_stride=slots, group_base=0, final=final)
        xs = _combine(st_s, ig_s, ye, x1, np_, ns, pad_w(w_s), modr, row_s, g_fin,
                      slot_stride=slots, group_base=ts, final=final)

    return (xp.reshape(bp, tp, d), xs.reshape(bs, tl, d),
            jnp.stack(st["ckv"], axis=1), jnp.stack(st["kr"], axis=1),
            jnp.stack(st["k"], axis=1), jnp.stack(st["v"], axis=1))
```

```python
import functools

import jax
import jax.numpy as jnp
from jax import lax
from jax.experimental import pallas as pl
from jax.experimental.pallas import tpu as pltpu

F32 = jnp.float32
BF16 = jnp.bfloat16

D_MODEL = 2048
GRID_W = 64
ROPE_THETA = 10000.0
EPS = 1e-6
MLA_HEADS = 16
MLA_Q_RANK = 512
MLA_KV_RANK = 256
MLA_NOPE = 128
MLA_ROPE = 64
MLA_V = 128
MLA_QK = MLA_NOPE + MLA_ROPE
MLA_SCALE = MLA_QK ** -0.5
GQA_HEADS = 16
GQA_KV_HEADS = 4
GQA_GROUP = GQA_HEADS // GQA_KV_HEADS
GQA_HEAD_DIM = 128
GQA_SCALE = GQA_HEAD_DIM ** -0.5
HEAD_V = 128
N_EXPERTS = 16
EC_FACTOR = 2
D_FF = 1024
LOG2E = 1.4426950408889634

LANES = 128
BF16_ROWS = 16
MXU_DIM = 256
MLA_QK_PAD = MXU_DIM
VMEM_LIMIT = 56 * 1024 * 1024

ROW_TM = 256
ATTN_TQ = 512
ATTN_TK = 512
COMBINE_TM = 256
COMBINE_W = 64
FFN_GATE_CHUNKS = 4
FFN_DOWN_CHUNKS = 4

C_CQ = 0
C_CKV = C_CQ + MLA_Q_RANK
C_KR = C_CKV + MLA_KV_RANK
C_GQ = C_KR + LANES
C_GK = C_GQ + GQA_HEADS * GQA_HEAD_DIM
C_GV = C_GK + GQA_KV_HEADS * GQA_HEAD_DIM
C_END = C_GV + GQA_KV_HEADS * GQA_HEAD_DIM


def _dot(a, b):
    return jnp.dot(a, b, preferred_element_type=F32)


def _dot_nt(a, b):
    return lax.dot_general(a, b, (((1,), (1,)), ((), ())), preferred_element_type=F32)


def _split(x):
    hi = x.astype(BF16)
    lo = (x - hi.astype(F32)).astype(BF16)
    return hi, lo


def _rms(x):
    return x * lax.rsqrt(jnp.mean(x * x, axis=-1, keepdims=True) + EPS)


def _params(sem):
    return pltpu.CompilerParams(dimension_semantics=sem, vmem_limit_bytes=VMEM_LIMIT)


def _const_spec(shape):
    nd = len(shape)
    return pl.BlockSpec(shape, lambda *_: (0,) * nd, pipeline_mode=pl.Buffered(1))


def _ada_kernel(c_ref, w_ref, b_ref, o_ref):
    c = c_ref[...]
    s = c / (1.0 + jnp.exp(-c))
    s_hi, s_lo = _split(s)
    w_hi, w_lo = _split(w_ref[...])
    o_ref[...] = _dot(s_hi, w_hi) + _dot(s_hi, w_lo) + _dot(s_lo, w_hi) + b_ref[...]


def _ada(cc, w_ada, b_ada):
    n_layers, d, n6 = w_ada.shape
    rows = cc.shape[0]
    tn = 1024
    return pl.pallas_call(
        _ada_kernel,
        grid=(n_layers, n6 // tn),
        in_specs=[
            pl.BlockSpec((rows, d), lambda l, j: (0, 0)),
            pl.BlockSpec((None, d, tn), lambda l, j: (l, 0, j)),
            pl.BlockSpec((None, 1, tn), lambda l, j: (l, 0, j)),
        ],
        out_specs=pl.BlockSpec((None, rows, tn), lambda l, j: (l, 0, j)),
        out_shape=jax.ShapeDtypeStruct((n_layers, rows, n6), F32),
        compiler_params=_params(("arbitrary", "arbitrary")),
        name="ada",
    )(cc, w_ada, b_ada.reshape(n_layers, 1, n6))


def _swap_halves(x, q):
    lane = lax.broadcasted_iota(jnp.int32, x.shape, 1)
    first = (lane % (2 * q)) < q
    return jnp.where(first, pltpu.roll(x, LANES - q, 1), pltpu.roll(x, q, 1))


def _qkv_kernel(rope, cache_out, x_ref, sh_ref, sc_ref, gn_ref, w_ref, gq_ref, wuq_ref, gkv_ref,
                ggq_ref, ggk_ref, cm_ref, sm_ref, cg_ref, sg_ref, *outs):
    qm_ref, ckv_ref, kr_ref, qg_ref, kg_ref, vg_ref = outs[:6]
    x = x_ref[...]
    h = (_rms(x) * gn_ref[...] * (1.0 + sc_ref[...]) + sh_ref[...]).astype(BF16)

    cq = _rms(_dot(h, w_ref[:, C_CQ:C_CKV])) * gq_ref[...]
    qm = _dot(cq.astype(BF16), wuq_ref[...]) * (MLA_SCALE * LOG2E)
    for hd in range(MLA_HEADS):
        lo = hd * MLA_QK_PAD
        qm_ref[hd, :, :LANES] = qm[:, lo:lo + LANES].astype(BF16)
        t = qm[:, lo + LANES:lo + 2 * LANES]
        if rope:
            t = t * cm_ref[...] + _swap_halves(t, MLA_ROPE // 4) * sm_ref[...]
        qm_ref[hd, :, LANES:] = t.astype(BF16)

    ckv = _rms(_dot(h, w_ref[:, C_CKV:C_KR])) * gkv_ref[...]
    ckv_ref[...] = ckv.astype(BF16)
    kr = _dot(h, w_ref[:, C_KR:C_GQ])
    if cache_out:
        outs[6][...] = ckv
        outs[7][...] = kr[:, :MLA_ROPE]
    if rope:
        kr = kr * cm_ref[...] + _swap_halves(kr, MLA_ROPE // 4) * sm_ref[...]
    kr_ref[...] = kr.astype(BF16)

    zq = _dot(h, w_ref[:, C_GQ:C_GK])
    for hd in range(GQA_HEADS):
        lo = hd * GQA_HEAD_DIM
        t = _rms(zq[:, lo:lo + GQA_HEAD_DIM]) * ggq_ref[...]
        if rope:
            t = t * cg_ref[...] + _swap_halves(t, GQA_HEAD_DIM // 4) * sg_ref[...]
        qg_ref[hd] = (t * (GQA_SCALE * LOG2E)).astype(BF16)
    zk = _dot(h, w_ref[:, C_GK:C_GV])
    for hd in range(GQA_KV_HEADS):
        lo = hd * GQA_HEAD_DIM
        t = _rms(zk[:, lo:lo + GQA_HEAD_DIM]) * ggk_ref[...]
        if cache_out:
            outs[8][:, lo:lo + GQA_HEAD_DIM] = t
        if rope:
            t = t * cg_ref[...] + _swap_halves(t, GQA_HEAD_DIM // 4) * sg_ref[...]
        kg_ref[:, lo:lo + GQA_HEAD_DIM] = t.astype(BF16)
    zv = _dot(h, w_ref[:, C_GV:C_END])
    vg_ref[...] = zv.astype(BF16)
    if cache_out:
        outs[9][...] = zv


def _qkv(x, t_len, modr, mod_row, lw, tabs, *, rope, cache_out):
    n, d = x.shape
    b = n // t_len
    tm = min(ROW_TM, t_len)
    tpb = t_len // tm
    pos_tiles = tabs[0].shape[0] // tm
    kvw = GQA_KV_HEADS * GQA_HEAD_DIM
    row = lambda w: pl.BlockSpec((tm, w), lambda i: (i, 0))
    row_in = row(d)
    heads = lambda nh, w: pl.BlockSpec((None, nh, tm, w), lambda i: (i // tpb, 0, i % tpb, 0))
    mod = lambda j: pl.BlockSpec((None, 1, d), lambda i: (mod_row(i * tm) * 6 + j, 0, 0))
    tab = lambda w: pl.BlockSpec((tm, w), lambda i: (i % pos_tiles, 0))
    in_specs = [
        row_in, mod(0), mod(1), _const_spec((1, d)), _const_spec((d, C_END)),
        _const_spec((1, MLA_Q_RANK)), _const_spec((MLA_Q_RANK, MLA_HEADS * MLA_QK_PAD)),
        _const_spec((1, MLA_KV_RANK)), _const_spec((1, GQA_HEAD_DIM)), _const_spec((1, GQA_HEAD_DIM)),
        tab(LANES), tab(LANES), tab(LANES), tab(LANES),
    ]
    out_shape = [
        jax.ShapeDtypeStruct((b, MLA_HEADS, t_len, MLA_QK_PAD), BF16),
        jax.ShapeDtypeStruct((n, MLA_KV_RANK), BF16),
        jax.ShapeDtypeStruct((n, LANES), BF16),
        jax.ShapeDtypeStruct((b, GQA_HEADS, t_len, GQA_HEAD_DIM), BF16),
        jax.ShapeDtypeStruct((n, kvw), BF16),
        jax.ShapeDtypeStruct((n, kvw), BF16),
    ]
    out_specs = [heads(MLA_HEADS, MLA_QK_PAD), row(MLA_KV_RANK), row(LANES),
                 heads(GQA_HEADS, GQA_HEAD_DIM), row(kvw), row(kvw)]
    if cache_out:
        out_shape += [
            jax.ShapeDtypeStruct((n, MLA_KV_RANK), F32),
            jax.ShapeDtypeStruct((n, MLA_ROPE), F32),
            jax.ShapeDtypeStruct((n, kvw), F32),
            jax.ShapeDtypeStruct((n, kvw), F32),
        ]
        out_specs += [row(MLA_KV_RANK), row(MLA_ROPE), row(kvw), row(kvw)]
    return pl.pallas_call(
        functools.partial(_qkv_kernel, rope, cache_out),
        grid=(n // tm,),
        in_specs=in_specs,
        out_specs=out_specs,
        out_shape=out_shape,
        compiler_params=_params(("arbitrary",)),
        name="qkv_rope" if rope else "qkv_ctx",
    )(x, modr, modr, lw["g_attn"], lw["w_a"], lw["g_mla_q"], lw["w_uq"], lw["g_mla_kv"],
      lw["g_gqa_q"], lw["g_gqa_k"], *tabs)


def _kv_expand_kernel(ckv_ref, kr_ref, wuk_ref, wuv_ref, k_ref, v_ref):
    ckv = ckv_ref[...]
    kn = _dot(ckv, wuk_ref[...]).astype(BF16)
    kr = kr_ref[...]
    for hd in range(MLA_HEADS):
        k_ref[:, hd * MLA_QK_PAD:hd * MLA_QK_PAD + LANES] = kn[:, hd * MLA_NOPE:(hd + 1) * MLA_NOPE]
        k_ref[:, hd * MLA_QK_PAD + LANES:(hd + 1) * MLA_QK_PAD] = kr
    v_ref[...] = _dot(ckv, wuv_ref[...]).astype(BF16)


def _kv_expand(ckv, kr, w_uk, w_uv):
    b, s, _ = ckv.shape
    ts = min(512, s)
    kw, vw = MLA_HEADS * MLA_QK_PAD, MLA_HEADS * MLA_V
    blk = lambda w: pl.BlockSpec((None, ts, w), lambda i, j: (i, j, 0))
    return pl.pallas_call(
        _kv_expand_kernel,
        grid=(b, s // ts),
        in_specs=[blk(MLA_KV_RANK), blk(LANES), _const_spec((MLA_KV_RANK, MLA_HEADS * MLA_NOPE)),
                  _const_spec((MLA_KV_RANK, vw))],
        out_specs=[blk(kw), blk(vw)],
        out_shape=[jax.ShapeDtypeStruct((b, s, kw), BF16), jax.ShapeDtypeStruct((b, s, vw), BF16)],
        compiler_params=_params(("arbitrary", "arbitrary")),
        name="kv_expand",
    )(ckv, kr, w_uk, w_uv)


def _attn_kernel(tq, tk, dq, hps, rolled, q_ref, k_ref, v_ref, o_ref, s0, s1, m0, m1, vx_ref):
    s_len = k_ref.shape[0]
    rows = q_ref.shape[1]
    n_chunks = s_len // tk
    dv = HEAD_V
    bufs = ((s0, m0), (s1, m1))

    for hh in range(hps):
        vx_ref[hh, :, :dv] = v_ref[:, hh * dv:(hh + 1) * dv]
        vx_ref[hh, :, dv:] = jnp.ones((s_len, dv), BF16)

    def stage(p1, p2):
        if p1 is not None:
            h1, r1, b1 = p1
            q = q_ref[h1, pl.ds(r1, tq), :]
            sb1, mb1 = bufs[b1]
        if p2 is not None:
            h2, r2, b2 = p2
            sb2, mb2 = bufs[b2]
            m = jnp.max(mb2[...], axis=1, keepdims=True)
            acc = None
        for c in range(n_chunks):
            ck = slice(c * tk, (c + 1) * tk)
            if p1 is not None:
                s = _dot_nt(q, k_ref[ck, h1 * dq:(h1 + 1) * dq])
                sb1[:, ck] = s
                mx = s[:, :LANES]
                for t in range(1, tk // LANES):
                    mx = jnp.maximum(mx, s[:, t * LANES:(t + 1) * LANES])
                mb1[...] = mx if c == 0 else jnp.maximum(mb1[...], mx)
            if p2 is not None:
                p = jnp.exp2(sb2[:, ck] - m).astype(BF16)
                part = _dot(p, vx_ref[h2, ck, :])
                acc = part if acc is None else acc + part
        if p2 is not None:
            o_ref[h2, pl.ds(r2, tq), :] = (acc[:, :dv] / acc[:, dv:]).astype(BF16)

    n_units = rows // tq
    if rolled:
        stage((0, 0, 0), None)

        def body(i, carry):
            u = 2 * i
            stage((0, pl.multiple_of((u + 1) * tq, tq), 1), (0, pl.multiple_of(u * tq, tq), 0))
            nxt = jnp.minimum(u + 2, n_units - 1)
            stage((0, pl.multiple_of(nxt * tq, tq), 0), (0, pl.multiple_of((u + 1) * tq, tq), 1))
            return carry

        lax.fori_loop(0, n_units // 2, body, 0)
    else:
        units = [(hh, u * tq) for hh in range(hps) for u in range(n_units)]
        for j in range(len(units) + 1):
            p1 = units[j] + (j % 2,) if j < len(units) else None
            p2 = units[j - 1] + ((j - 1) % 2,) if j > 0 else None
            stage(p1, p2)


def _attn(q, k, v, *, hps, rolled):
    b, hk, rows, dq = q.shape
    s = k.shape[1]
    dv = HEAD_V
    tq = min(ATTN_TQ, rows)
    tk = min(ATTN_TK, s)
    assert rows % tq == 0 and s % tk == 0 and hk % hps == 0
    assert not rolled or (hps == 1 and (rows // tq) % 2 == 0)
    return pl.pallas_call(
        functools.partial(_attn_kernel, tq, tk, dq, hps, rolled),
        grid=(b, hk // hps),
        in_specs=[
            pl.BlockSpec((None, hps, rows, dq), lambda i, h: (i, h, 0, 0)),
            pl.BlockSpec((None, s, hps * dq), lambda i, h: (i, 0, h)),
            pl.BlockSpec((None, s, hps * dv), lambda i, h: (i, 0, h)),
        ],
        out_specs=pl.BlockSpec((None, hps, rows, dv), lambda i, h: (i, h, 0, 0)),
        out_shape=jax.ShapeDtypeStruct((b, hk, rows, dv), BF16),
        scratch_shapes=[pltpu.VMEM((tq, s), F32), pltpu.VMEM((tq, s), F32),
                        pltpu.VMEM((tq, LANES), F32), pltpu.VMEM((tq, LANES), F32),
                        pltpu.VMEM((hps, s, 2 * dv), BF16)],
        compiler_params=_params(("arbitrary", "arbitrary")),
        name="attn_rolled" if rolled else "attn_flat",
    )(q, k, v)


def _out_proj_kernel(x_ref, oa_ref, ob_ref, sh1_ref, sc1_ref, gt1_ref, sh2_ref, sc2_ref, gn1_ref,
                     gn2_ref, wgl_ref, bg_ref, wo_ref, wrh_ref, wrl_ref, x1_ref, aff_ref):
    d = x_ref.shape[1]
    x = x_ref[...]
    h = (_rms(x) * gn1_ref[...] * (1.0 + sc1_ref[...]) + sh1_ref[...]).astype(BF16)
    ga = 1.0 / (1.0 + jnp.exp(-(_dot(h, wgl_ref[:, :d]) + bg_ref[:, :d])))
    gb = 1.0 / (1.0 + jnp.exp(-(_dot(h, wgl_ref[:, d:]) + bg_ref[:, d:])))
    parts = []
    for hd in range(d // HEAD_V):
        cs = slice(hd * HEAD_V, (hd + 1) * HEAD_V)
        parts.append((ga[:, cs] * oa_ref[hd].astype(F32) + gb[:, cs] * ob_ref[hd].astype(F32)).astype(BF16))
    merged = jnp.concatenate(parts, axis=1)
    x1 = x + gt1_ref[...] * _dot(merged, wo_ref[...])
    x1_ref[...] = x1
    h2 = _rms(x1) * gn2_ref[...] * (1.0 + sc2_ref[...]) + sh2_ref[...]
    h_hi, h_lo = _split(h2)
    lg = _dot_nt(wrh_ref[...], h_hi) + _dot_nt(wrh_ref[...], h_lo) + _dot_nt(wrl_ref[...], h_hi)
    e = jnp.exp(lg - jnp.max(lg, axis=0, keepdims=True))
    aff_ref[...] = e / jnp.sum(e, axis=0, keepdims=True)


def _out_proj(x, t_len, oa, ob, modr, mod_row, lw):
    n, d = x.shape
    tm = min(ROW_TM, t_len)
    tpb = t_len // tm
    nh = d // HEAD_V
    row = lambda: pl.BlockSpec((tm, d), lambda i: (i, 0))
    heads = lambda: pl.BlockSpec((None, nh, tm, HEAD_V), lambda i: (i // tpb, 0, i % tpb, 0))
    mod = lambda j: pl.BlockSpec((None, 1, d), lambda i: (mod_row(i * tm) * 6 + j, 0, 0))
    return pl.pallas_call(
        _out_proj_kernel,
        grid=(n // tm,),
        in_specs=[row(), heads(), heads(), mod(0), mod(1), mod(2), mod(3), mod(4),
                  _const_spec((1, d)), _const_spec((1, d)), _const_spec((d, 2 * d)),
                  _const_spec((1, 2 * d)), _const_spec((d, d)),
                  _const_spec((N_EXPERTS, d)), _const_spec((N_EXPERTS, d))],
        out_specs=[row(), pl.BlockSpec((N_EXPERTS, tm), lambda i: (0, i))],
        out_shape=[jax.ShapeDtypeStruct((n, d), F32), jax.ShapeDtypeStruct((N_EXPERTS, n), F32)],
        compiler_params=_params(("arbitrary",)),
        name="out_proj",
    )(x, oa, ob, modr, modr, modr, modr, modr, lw["g_attn"], lw["g_ffn"], lw["w_gl"], lw["b_gate"],
      lw["w_out"], lw["wr_hi"], lw["wr_lo"])


def _lane_cumsum(x_bf16, tri):
    t = x_bf16.shape[1]
    blk = tri.shape[0]
    parts = []
    carry = jnp.zeros((x_bf16.shape[0], 1), F32)
    for c in range(t // blk):
        part = _dot(x_bf16[:, c * blk:(c + 1) * blk], tri) + carry
        carry = part[:, blk - 1:blk]
        parts.append(part)
    return parts[0] if len(parts) == 1 else jnp.concatenate(parts, axis=1)


def _router_kernel(cap, tok_base, aff_ref, idx_ref, w_ref, pos_ref):
    r = pl.program_id(0)
    a = aff_ref[...]
    n_e, t = a.shape
    bits = pltpu.bitcast(a, jnp.int32)

    thr = jnp.zeros((n_e, 1), jnp.int32)
    for b in range(30, -1, -1):
        cand = thr | (1 << b)
        cnt = jnp.sum((bits >= cand).astype(jnp.int32), axis=1, keepdims=True)
        thr = jnp.where(cnt >= cap, cand, thr)
    gt = bits > thr
    eq = bits == thr
    need = (cap - jnp.sum(gt.astype(jnp.int32), axis=1, keepdims=True)).astype(F32)

    blk = min(MXU_DIM, t)
    ri = lax.broadcasted_iota(jnp.int32, (blk, blk), 0)
    ci = lax.broadcasted_iota(jnp.int32, (blk, blk), 1)
    tri = (ri <= ci).astype(BF16)
    cum_eq = _lane_cumsum(eq.astype(BF16), tri)
    sel = gt | (eq & (cum_eq <= need))
    cnt_incl = _lane_cumsum(sel.astype(BF16), tri)
    w_ref[...] = jnp.where(sel, a, 0.0)
    pos_ref[...] = (cnt_incl - sel.astype(F32)).astype(jnp.int32) + r * cap

    ts = min(LANES, cap)
    ones = jnp.ones((t, LANES), BF16)
    lane = lax.broadcasted_iota(jnp.int32, (1, LANES), 1)
    for sb in range(cap // ts):
        s_iota = (lax.broadcasted_iota(jnp.int32, (ts, 1), 0) + sb * ts).astype(F32)
        out = jnp.zeros((ts, LANES), F32)
        for e in range(n_e):
            cmp = (cnt_incl[e:e + 1, :] <= s_iota).astype(BF16)
            out = out + jnp.where(lane == e, _dot(cmp, ones), 0.0)
        idx_ref[sb * ts:(sb + 1) * ts, :] = out.astype(jnp.int32) + (r * t + tok_base)


def _router(aff_t, n_req, tok_base):
    n_e, n = aff_t.shape
    t = n // n_req
    cap = EC_FACTOR * t // N_EXPERTS
    return pl.pallas_call(
        functools.partial(_router_kernel, cap, tok_base),
        grid=(n_req,),
        in_specs=[pl.BlockSpec((n_e, t), lambda r: (0, r))],
        out_specs=[pl.BlockSpec((cap, LANES), lambda r: (r, 0)),
                   pl.BlockSpec((n_e, t), lambda r: (0, r)),
                   pl.BlockSpec((n_e, t), lambda r: (0, r))],
        out_shape=[jax.ShapeDtypeStruct((n_req * cap, LANES), jnp.int32),
                   jax.ShapeDtypeStruct((n_e, n), F32),
                   jax.ShapeDtypeStruct((n_e, n), jnp.int32)],
        compiler_params=_params(("arbitrary",)),
        name="router",
    )(aff_t)


def _ffn_kernel(ts, idx_ref, x_hbm, sh_ref, sc_ref, gn_ref, wg_ref, wu_ref, wd_ref, ye_ref, xbuf, sem):
    n_tiles = pl.num_programs(1)
    step = pl.program_id(0) * n_tiles + pl.program_id(1)
    last = pl.num_programs(0) * n_tiles - 1
    slot = step % 2
    nslot = 1 - slot
    d = xbuf.shape[2]

    def row_copy(tile, s, buf):
        return pltpu.make_async_copy(x_hbm.at[pl.ds(idx_ref[tile * ts + s], 1), :],
                                     xbuf.at[buf, pl.ds(s, 1), :], sem.at[buf])

    def wait_tile(buf):
        pltpu.make_async_copy(x_hbm.at[pl.ds(0, ts), :], xbuf.at[buf], sem.at[buf]).wait()

    @pl.when(step == 0)
    def _():
        def body(s, carry):
            row_copy(0, s, 0).start()
            return carry
        lax.fori_loop(0, ts, body, 0, unroll=8)

    wait_tile(slot)
    nxt = jnp.minimum(step + 1, last)
    n_groups = FFN_GATE_CHUNKS + FFN_DOWN_CHUNKS
    per_group = ts // n_groups
    issued = [0]

    def issue_group():
        for s in range(issued[0], issued[0] + per_group):
            row_copy(nxt, s, nslot).start()
        issued[0] += per_group

    h = (_rms(xbuf[slot]) * gn_ref[...] * (1.0 + sc_ref[...]) + sh_ref[...]).astype(BF16)
    fw = D_FF // FFN_GATE_CHUNKS
    hid = []
    for c in range(FFN_GATE_CHUNKS):
        gate = _dot(h, wg_ref[:, c * fw:(c + 1) * fw])
        up = _dot(h, wu_ref[:, c * fw:(c + 1) * fw])
        hid.append((gate / (1.0 + jnp.exp(-gate)) * up).astype(BF16))
        issue_group()
    hid = jnp.concatenate(hid, axis=1)
    dw = d // FFN_DOWN_CHUNKS
    for c in range(FFN_DOWN_CHUNKS):
        ye_ref[:, c * dw:(c + 1) * dw] = _dot(hid, wd_ref[:, c * dw:(c + 1) * dw]).astype(BF16)
        issue_group()

    @pl.when(step == last)
    def _():
        wait_tile(nslot)


def _ffn(idx, x1, modr, layer, lw, n_tiles, ts):
    d = x1.shape[1]
    assert ts % (FFN_GATE_CHUNKS + FFN_DOWN_CHUNKS) == 0
    assert D_FF % FFN_GATE_CHUNKS == 0 and d % FFN_DOWN_CHUNKS == 0
    mod = lambda c: pl.BlockSpec((None, 1, d), lambda e, j, *_: ((layer * 8 + j) * 6 + c, 0, 0))
    grid_spec = pltpu.PrefetchScalarGridSpec(
        num_scalar_prefetch=1,
        grid=(N_EXPERTS, n_tiles),
        in_specs=[
            pl.BlockSpec(memory_space=pl.ANY), mod(3), mod(4),
            pl.BlockSpec((1, d), lambda e, j, *_: (0, 0)),
            pl.BlockSpec((None, d, D_FF), lambda e, j, *_: (e, 0, 0)),
            pl.BlockSpec((None, d, D_FF), lambda e, j, *_: (e, 0, 0)),
            pl.BlockSpec((None, D_FF, d), lambda e, j, *_: (e, 0, 0)),
        ],
        out_specs=pl.BlockSpec((None, ts, d), lambda e, j, *_: (e, j, 0)),
        scratch_shapes=[pltpu.VMEM((2, ts, d), F32), pltpu.SemaphoreType.DMA((2,))],
    )
    return pl.pallas_call(
        functools.partial(_ffn_kernel, ts),
        grid_spec=grid_spec,
        out_shape=jax.ShapeDtypeStruct((N_EXPERTS, n_tiles * ts, d), BF16),
        compiler_params=_params(("arbitrary", "arbitrary")),
        name="ffn",
    )(idx, x1, modr, modr, lw["g_ffn"], lw["w_gate"], lw["w_up"], lw["w_down"])


def _combine_kernel(final, tm, slots, starts_ref, ye_hbm, x_ref, pos_ref, w_ref, gt_ref, gf_ref, o_ref,
                    stage, stage_x, acc_ref, sem, sem_x):
    i = pl.program_id(0)
    last = pl.num_programs(0) - 1
    n_e, wn = N_EXPERTS, COMBINE_W
    per_pass = MXU_DIM // wn
    top = n_e * slots - wn
    slot = i % 2
    nslot = 1 - slot

    def window(tile, e, c):
        s_lo = starts_ref[tile * n_e + e]
        nominal = e * slots + (s_lo // BF16_ROWS) * BF16_ROWS + c * wn
        return jnp.minimum(nominal, top), nominal

    def win_copy(tile, e, c, dst, s):
        base, _ = window(tile, e, c)
        return pltpu.make_async_copy(ye_hbm.at[pl.ds(pl.multiple_of(base, BF16_ROWS), wn), :],
                                     dst.at[pl.ds(e * wn, wn), :], s)

    @pl.when(i == 0)
    def _():
        for e in range(n_e):
            win_copy(0, e, 0, stage.at[0], sem.at[0]).start()

    pltpu.make_async_copy(ye_hbm.at[pl.ds(0, n_e * wn), :], stage.at[slot], sem.at[slot]).wait()
    nxt = jnp.minimum(i + 1, last)
    for e in range(n_e):
        win_copy(nxt, e, 0, stage.at[nslot], sem.at[nslot]).start()

    pos = pos_ref[...]
    w = w_ref[...]
    lane = lax.broadcasted_iota(jnp.int32, (1, MXU_DIM), 1)

    def accumulate(src, c, first):
        for g in range(n_e // per_pass):
            tgt = val = None
            for j in range(per_pass):
                e = g * per_pass + j
                base, nominal = window(i, e, c)
                gs = pos[:, e:e + 1] + e * slots
                t_j = gs - (base - j * wn)
                in_round = (gs >= nominal) & (gs < nominal + wn)
                v_j = jnp.where(in_round, w[:, e:e + 1], 0.0)
                if j == 0:
                    tgt, val = t_j, v_j
                else:
                    here = lane >= j * wn
                    tgt = jnp.where(here, t_j, tgt)
                    val = jnp.where(here, v_j, val)
            place = jnp.where(tgt == lane, val, 0.0).astype(BF16)
            part = _dot(place, src[g * MXU_DIM:(g + 1) * MXU_DIM, :])
            if first and g == 0:
                acc_ref[...] = part
            else:
                acc_ref[...] += part

    accumulate(stage.at[slot], 0, True)

    rounds = 1
    for e in range(n_e):
        s_lo = starts_ref[i * n_e + e]
        span = starts_ref[(i + 1) * n_e + e] - (s_lo // BF16_ROWS) * BF16_ROWS
        rounds = jnp.maximum(rounds, (span + wn - 1) // wn)

    def extra_round(c, carry):
        for e in range(n_e):
            win_copy(i, e, c, stage_x, sem_x).start()
        pltpu.make_async_copy(ye_hbm.at[pl.ds(0, n_e * wn), :], stage_x, sem_x).wait()
        accumulate(stage_x, c, False)
        return carry

    lax.fori_loop(1, rounds, extra_round, 0)

    y = x_ref[...] + gt_ref[...] * acc_ref[...]
    if final:
        y = _rms(y) * gf_ref[...]
    o_ref[...] = y

    @pl.when(i == last)
    def _():
        pltpu.make_async_copy(ye_hbm.at[pl.ds(0, n_e * wn), :], stage.at[nslot], sem.at[nslot]).wait()


def _combine(starts, ye, x1, row_off, n, pos, w, modr, mod_row, g_final, *, slots, final):
    d = x1.shape[1]
    tm = COMBINE_TM
    off = row_off // tm
    assert MXU_DIM % COMBINE_W == 0 and N_EXPERTS % (MXU_DIM // COMBINE_W) == 0
    grid_spec = pltpu.PrefetchScalarGridSpec(
        num_scalar_prefetch=1,
        grid=(n // tm,),
        in_specs=[
            pl.BlockSpec(memory_space=pl.ANY),
            pl.BlockSpec((tm, d), lambda i, *_: (i + off, 0)),
            pl.BlockSpec((tm, LANES), lambda i, *_: (i, 0)),
            pl.BlockSpec((tm, LANES), lambda i, *_: (i, 0)),
            pl.BlockSpec((None, 1, d), lambda i, *_: (mod_row(i * tm) * 6 + 5, 0, 0)),
            pl.BlockSpec((1, d), lambda i, *_: (0, 0)),
        ],
        out_specs=pl.BlockSpec((tm, d), lambda i, *_: (i, 0)),
        scratch_shapes=[pltpu.VMEM((2, N_EXPERTS * COMBINE_W, d), BF16),
                        pltpu.VMEM((N_EXPERTS * COMBINE_W, d), BF16),
                        pltpu.VMEM((tm, d), F32),
                        pltpu.SemaphoreType.DMA((2,)), pltpu.SemaphoreType.DMA(())],
    )
    return pl.pallas_call(
        functools.partial(_combine_kernel, final, tm, slots),
        grid_spec=grid_spec,
        out_shape=jax.ShapeDtypeStruct((n, d), F32),
        compiler_params=_params(("arbitrary",)),
        name="combine_final" if final else "combine",
    )(starts, ye, x1, pos, w, modr, g_final)


def _rope_tables(n_tokens, rot_dim):
    t = jnp.arange(n_tokens)
    row = (t // GRID_W).astype(F32)
    col = (t % GRID_W).astype(F32)
    quarter = rot_dim // 4
    inv = ROPE_THETA ** (-jnp.arange(quarter, dtype=F32) / quarter)
    ar, ac = row[:, None] * inv, col[:, None] * inv
    cos = jnp.concatenate([jnp.cos(ar), jnp.cos(ar), jnp.cos(ac), jnp.cos(ac)], axis=-1)
    sin = jnp.concatenate([-jnp.sin(ar), jnp.sin(ar), -jnp.sin(ac), jnp.sin(ac)], axis=-1)
    pad = LANES - rot_dim
    if pad:
        cos = jnp.concatenate([cos, jnp.ones((n_tokens, pad), F32)], axis=-1)
        sin = jnp.concatenate([sin, jnp.zeros((n_tokens, pad), F32)], axis=-1)
    return cos, sin


def _layer_weights(l, g_attn_norm, w_in, b_gate, g_mla_q, w_mla_uq, g_mla_kv, w_mla_uk, w_mla_uv,
                   g_gqa_q, g_gqa_k, w_out, g_ffn_norm, w_router, w_gate_e, w_up_e, w_down_e):
    d = D_MODEL
    o_ckv = MLA_Q_RANK + MLA_KV_RANK
    o_kr = o_ckv + MLA_ROPE
    o_gl = o_kr + (GQA_HEADS + 2 * GQA_KV_HEADS) * GQA_HEAD_DIM
    w = w_in[l]
    w_a = jnp.concatenate(
        [w[:, :o_kr], jnp.zeros((d, LANES - MLA_ROPE), F32), w[:, o_kr:o_gl]], axis=1).astype(BF16)
    w_uq = w_mla_uq[l].reshape(MLA_Q_RANK, MLA_HEADS, MLA_QK)
    w_uq = jnp.pad(w_uq, ((0, 0), (0, 0), (0, MLA_QK_PAD - MLA_QK))).reshape(MLA_Q_RANK, -1).astype(BF16)
    wr = w_router[l].T
    wr_hi = wr.astype(BF16)
    wr_lo = (wr - wr_hi.astype(F32)).astype(BF16)
    return {
        "g_attn": g_attn_norm[l][None], "g_ffn": g_ffn_norm[l][None], "w_a": w_a,
        "g_mla_q": g_mla_q[l][None], "w_uq": w_uq, "g_mla_kv": g_mla_kv[l][None],
        "g_gqa_q": g_gqa_q[l][None], "g_gqa_k": g_gqa_k[l][None],
        "w_uk": w_mla_uk[l].astype(BF16), "w_uv": w_mla_uv[l].astype(BF16),
        "w_gl": w[:, o_gl:].astype(BF16), "b_gate": b_gate[l][None], "w_out": w_out[l].astype(BF16),
        "wr_hi": wr_hi, "wr_lo": wr_lo,
        "w_gate": w_gate_e[l].astype(BF16), "w_up": w_up_e[l].astype(BF16),
        "w_down": w_down_e[l].astype(BF16),
    }


def _combine_tables(pos, w, slot_base, n_slots):
    pos = pos + slot_base
    starts = jnp.concatenate([pos[:, ::COMBINE_TM].T,
                              jnp.full((1, N_EXPERTS), slot_base + n_slots, jnp.int32)], axis=0)
    lanes = lambda a: jnp.pad(a.T, ((0, 0), (0, LANES - N_EXPERTS)))
    return starts.reshape(-1), lanes(pos), lanes(w)


def kernel(x_prompt, x_sample, c, cache_mla_ckv, cache_mla_krope, cache_gqa_k, cache_gqa_v, c_ctx,
           w_ada, b_ada, g_attn_norm, w_in, b_gate, g_mla_q, w_mla_uq, g_mla_kv, w_mla_uk, w_mla_uv,
           g_gqa_q, g_gqa_k, w_out, g_ffn_norm, w_router, w_gate_e, w_up_e, w_down_e, g_final):
    bp, tp, d = x_prompt.shape
    bs, tl, _ = x_sample.shape
    n_layers = w_ada.shape[0]
    np_, ns = bp * tp, bs * tl
    kvw = GQA_KV_HEADS * GQA_HEAD_DIM
    cap_p = EC_FACTOR * tp // N_EXPERTS
    cap_s = EC_FACTOR * tl // N_EXPERTS
    ts = cap_s
    assert bp * cap_p == ts and bs + 1 <= 8
    n_tiles = 1 + bs
    slots = n_tiles * ts

    cc = jnp.zeros((8, d), F32).at[0].set(c_ctx).at[1:1 + bs].set(c)
    modr = _ada(cc, w_ada, b_ada).reshape(n_layers * 8 * 6, 1, d)

    cos_m, sin_m = _rope_tables(tl, MLA_ROPE)
    cos_g, sin_g = _rope_tables(tl, GQA_HEAD_DIM)
    tabs = (cos_m, sin_m, cos_g, sin_g)
    g_fin = g_final[None]

    xp = x_prompt.reshape(np_, d)
    xs = x_sample.reshape(ns, d)
    st = {"ckv": [], "kr": [], "k": [], "v": []}
    for l in range(n_layers):
        lw = _layer_weights(l, g_attn_norm, w_in, b_gate, g_mla_q, w_mla_uq, g_mla_kv, w_mla_uk,
                            w_mla_uv, g_gqa_q, g_gqa_k, w_out, g_ffn_norm, w_router, w_gate_e,
                            w_up_e, w_down_e)
        row_p = lambda r, l=l: l * 8
        row_s = lambda r, l=l: l * 8 + 1 + r // tl

        qm, ckv, kr, qg, kg, vg, ckv32, kr32, kg32, vg32 = _qkv(
            xp, tp, modr, row_p, lw, tabs, rope=False, cache_out=True)
        st["ckv"].append(ckv32.reshape(bp, tp, MLA_KV_RANK))
        st["kr"].append(kr32.reshape(bp, tp, MLA_ROPE))
        st["k"].append(kg32.reshape(bp, tp, GQA_KV_HEADS, GQA_HEAD_DIM))
        st["v"].append(vg32.reshape(bp, tp, GQA_KV_HEADS, GQA_HEAD_DIM))
        k_m, v_m = _kv_expand(ckv.reshape(bp, tp, -1), kr.reshape(bp, tp, -1), lw["w_uk"], lw["w_uv"])
        oa_p = _attn(qm, k_m, v_m, hps=MLA_HEADS, rolled=False)
        ob_p = _attn(qg.reshape(bp, GQA_KV_HEADS, GQA_GROUP * tp, GQA_HEAD_DIM), kg.reshape(bp, tp, -1),
                     vg.reshape(bp, tp, -1), hps=GQA_KV_HEADS, rolled=False)
        x1p, aff_p = _out_proj(xp, tp, oa_p, ob_p.reshape(bp, GQA_HEADS, tp, HEAD_V), modr, row_p, lw)

        qm, ckv, kr, qg, kg, vg = _qkv(xs, tl, modr, row_s, lw, tabs, rope=True, cache_out=False)
        kr_cache = jnp.pad(cache_mla_krope[:, l], ((0, 0), (0, 0), (0, LANES - MLA_ROPE))).astype(BF16)
        ckv_all = jnp.concatenate([cache_mla_ckv[:, l].astype(BF16), ckv.reshape(bs, tl, -1)], axis=1)
        kr_all = jnp.concatenate([kr_cache, kr.reshape(bs, tl, -1)], axis=1)
        k_m, v_m = _kv_expand(ckv_all, kr_all, lw["w_uk"], lw["w_uv"])
        oa_s = _attn(qm, k_m, v_m, hps=1, rolled=True)
        past = cache_gqa_k.shape[2]
        k_all = jnp.concatenate([cache_gqa_k[:, l].reshape(bs, past, kvw).astype(BF16),
                                 kg.reshape(bs, tl, kvw)], axis=1)
        v_all = jnp.concatenate([cache_gqa_v[:, l].reshape(bs, past, kvw).astype(BF16),
                                 vg.reshape(bs, tl, kvw)], axis=1)
        ob_s = _attn(qg.reshape(bs, GQA_KV_HEADS, GQA_GROUP * tl, GQA_HEAD_DIM), k_all, v_all,
                     hps=1, rolled=True)
        x1s, aff_s = _out_proj(xs, tl, oa_s, ob_s.reshape(bs, GQA_HEADS, tl, HEAD_V), modr, row_s, lw)

        x1 = jnp.concatenate([x1p, x1s], axis=0)
        idx_p, w_p, pos_p = _router(aff_p, bp, 0)
        idx_s, w_s, pos_s = _router(aff_s, bs, np_)
        idx = jnp.concatenate([idx_p[:, :N_EXPERTS], idx_s[:, :N_EXPERTS]], axis=0).T.reshape(-1)
        ye = _ffn(idx, x1, modr, l, lw, n_tiles, ts).reshape(N_EXPERTS * slots, d)
        final = l == n_layers - 1
        st_p, ps_p, wt_p = _combine_tables(pos_p, w_p, 0, bp * cap_p)
        st_s, ps_s, wt_s = _combine_tables(pos_s, w_s, ts, bs * cap_s)
        xp = _combine(st_p, ye, x1, 0, np_, ps_p, wt_p, modr, row_p, g_fin, slots=slots, final=final)
        xs = _combine(st_s, ye, x1, np_, ns, ps_s, wt_s, modr, row_s, g_fin, slots=slots, final=final)

    return (xp.reshape(bp, tp, d), xs.reshape(bs, tl, d),
            jnp.stack(st["ckv"], axis=1), jnp.stack(st["kr"], axis=1),
            jnp.stack(st["k"], axis=1), jnp.stack(st["v"], axis=1))
```

```python
import functools

import jax
import jax.numpy as jnp
from jax import lax
from jax.experimental import pallas as pl
from jax.experimental.pallas import tpu as pltpu

F32 = jnp.float32
BF16 = jnp.bfloat16

D_MODEL = 2048
GRID_W = 64
ROPE_THETA = 10000.0
EPS = 1e-6
MLA_HEADS = 16
MLA_Q_RANK = 512
MLA_KV_RANK = 256
MLA_NOPE = 128
MLA_ROPE = 64
MLA_V = 128
MLA_QK = MLA_NOPE + MLA_ROPE
MLA_SCALE = MLA_QK ** -0.5
GQA_HEADS = 16
GQA_KV_HEADS = 4
GQA_GROUP = GQA_HEADS // GQA_KV_HEADS
GQA_HEAD_DIM = 128
GQA_SCALE = GQA_HEAD_DIM ** -0.5
HEAD_V = 128
N_EXPERTS = 16
EC_FACTOR = 2
D_FF = 1024
LOG2E = 1.4426950408889634

LANES = 128
BF16_ROWS = 16
MXU_DIM = 256
MLA_QK_PAD = MXU_DIM
VMEM_LIMIT = 56 * 1024 * 1024

ROW_TM = 256
ATTN_TQ = 512
ATTN_TK = 512
COMBINE_TM = 256
COMBINE_W = 64
FFN_GATE_CHUNKS = 4
FFN_DOWN_CHUNKS = 4

C_CQ = 0
C_CKV = C_CQ + MLA_Q_RANK
C_KR = C_CKV + MLA_KV_RANK
C_GQ = C_KR + LANES
C_GK = C_GQ + GQA_HEADS * GQA_HEAD_DIM
C_GV = C_GK + GQA_KV_HEADS * GQA_HEAD_DIM
C_END = C_GV + GQA_KV_HEADS * GQA_HEAD_DIM


def _dot(a, b):
    return jnp.dot(a, b, preferred_element_type=F32)


def _dot_nt(a, b):
    return lax.dot_general(a, b, (((1,), (1,)), ((), ())), preferred_element_type=F32)


def _split(x):
    hi = x.astype(BF16)
    lo = (x - hi.astype(F32)).astype(BF16)
    return hi, lo


def _rms(x):
    return x * lax.rsqrt(jnp.mean(x * x, axis=-1, keepdims=True) + EPS)


def _params(sem):
    return pltpu.CompilerParams(dimension_semantics=sem, vmem_limit_bytes=VMEM_LIMIT)


def _const_spec(shape):
    nd = len(shape)
    return pl.BlockSpec(shape, lambda *_: (0,) * nd, pipeline_mode=pl.Buffered(1))


def _ada_kernel(c_ref, w_ref, b_ref, o_ref):
    c = c_ref[...]
    s = c / (1.0 + jnp.exp(-c))
    s_hi, s_lo = _split(s)
    w_hi, w_lo = _split(w_ref[...])
    o_ref[...] = _dot(s_hi, w_hi) + _dot(s_hi, w_lo) + _dot(s_lo, w_hi) + b_ref[...]


def _ada(cc, w_ada, b_ada):
    n_layers, d, n6 = w_ada.shape
    rows = cc.shape[0]
    tn = 1024
    return pl.pallas_call(
        _ada_kernel,
        grid=(n_layers, n6 // tn),
        in_specs=[
            pl.BlockSpec((rows, d), lambda l, j: (0, 0)),
            pl.BlockSpec((None, d, tn), lambda l, j: (l, 0, j)),
            pl.BlockSpec((None, 1, tn), lambda l, j: (l, 0, j)),
        ],
        out_specs=pl.BlockSpec((None, rows, tn), lambda l, j: (l, 0, j)),
        out_shape=jax.ShapeDtypeStruct((n_layers, rows, n6), F32),
        compiler_params=_params(("arbitrary", "arbitrary")),
        name="ada",
    )(cc, w_ada, b_ada.reshape(n_layers, 1, n6))


def _swap_halves(x, q):
    lane = lax.broadcasted_iota(jnp.int32, x.shape, 1)
    first = (lane % (2 * q)) < q
    return jnp.where(first, pltpu.roll(x, LANES - q, 1), pltpu.roll(x, q, 1))


def _qkv_kernel(rope, cache_out, x_ref, sh_ref, sc_ref, gn_ref, w_ref, gq_ref, wuq_ref, gkv_ref,
                ggq_ref, ggk_ref, cm_ref, sm_ref, cg_ref, sg_ref, *outs):
    qm_ref, ckv_ref, kr_ref, qg_ref, kg_ref, vg_ref = outs[:6]
    x = x_ref[...]
    h = (_rms(x) * gn_ref[...] * (1.0 + sc_ref[...]) + sh_ref[...]).astype(BF16)

    cq = _rms(_dot(h, w_ref[:, C_CQ:C_CKV])) * gq_ref[...]
    qm = _dot(cq.astype(BF16), wuq_ref[...]) * (MLA_SCALE * LOG2E)
    for hd in range(MLA_HEADS):
        lo = hd * MLA_QK_PAD
        qm_ref[hd, :, :LANES] = qm[:, lo:lo + LANES].astype(BF16)
        t = qm[:, lo + LANES:lo + 2 * LANES]
        if rope:
            t = t * cm_ref[...] + _swap_halves(t, MLA_ROPE // 4) * sm_ref[...]
        qm_ref[hd, :, LANES:] = t.astype(BF16)

    ckv = _rms(_dot(h, w_ref[:, C_CKV:C_KR])) * gkv_ref[...]
    ckv_ref[...] = ckv.astype(BF16)
    kr = _dot(h, w_ref[:, C_KR:C_GQ])
    if cache_out:
        outs[6][...] = ckv
        outs[7][...] = kr[:, :MLA_ROPE]
    if rope:
        kr = kr * cm_ref[...] + _swap_halves(kr, MLA_ROPE // 4) * sm_ref[...]
    kr_ref[...] = kr.astype(BF16)

    zq = _dot(h, w_ref[:, C_GQ:C_GK])
    for hd in range(GQA_HEADS):
        lo = hd * GQA_HEAD_DIM
        t = _rms(zq[:, lo:lo + GQA_HEAD_DIM]) * ggq_ref[...]
        if rope:
            t = t * cg_ref[...] + _swap_halves(t, GQA_HEAD_DIM // 4) * sg_ref[...]
        qg_ref[hd] = (t * (GQA_SCALE * LOG2E)).astype(BF16)
    zk = _dot(h, w_ref[:, C_GK:C_GV])
    for hd in range(GQA_KV_HEADS):
        lo = hd * GQA_HEAD_DIM
        t = _rms(zk[:, lo:lo + GQA_HEAD_DIM]) * ggk_ref[...]
        if cache_out:
            outs[8][:, lo:lo + GQA_HEAD_DIM] = t
        if rope:
            t = t * cg_ref[...] + _swap_halves(t, GQA_HEAD_DIM // 4) * sg_ref[...]
        kg_ref[:, lo:lo + GQA_HEAD_DIM] = t.astype(BF16)
    zv = _dot(h, w_ref[:, C_GV:C_END])
    vg_ref[...] = zv.astype(BF16)
    if cache_out:
        outs[9][...] = zv


def _qkv(x, t_len, modr, mod_row, lw, tabs, *, rope, cache_out):
    n, d = x.shape
    b = n // t_len
    tm = min(ROW_TM, t_len)
    tpb = t_len // tm
    pos_tiles = tabs[0].shape[0] // tm
    kvw = GQA_KV_HEADS * GQA_HEAD_DIM
    row = lambda w: pl.BlockSpec((tm, w), lambda i: (i, 0))
    row_in = row(d)
    heads = lambda nh, w: pl.BlockSpec((None, nh, tm, w), lambda i: (i // tpb, 0, i % tpb, 0))
    mod = lambda j: pl.BlockSpec((None, 1, d), lambda i: (mod_row(i * tm) * 6 + j, 0, 0))
    tab = lambda w: pl.BlockSpec((tm, w), lambda i: (i % pos_tiles, 0))
    in_specs = [
        row_in, mod(0), mod(1), _const_spec((1, d)), _const_spec((d, C_END)),
        _const_spec((1, MLA_Q_RANK)), _const_spec((MLA_Q_RANK, MLA_HEADS * MLA_QK_PAD)),
        _const_spec((1, MLA_KV_RANK)), _const_spec((1, GQA_HEAD_DIM)), _const_spec((1, GQA_HEAD_DIM)),
        tab(LANES), tab(LANES), tab(LANES), tab(LANES),
    ]
    out_shape = [
        jax.ShapeDtypeStruct((b, MLA_HEADS, t_len, MLA_QK_PAD), BF16),
        jax.ShapeDtypeStruct((n, MLA_KV_RANK), BF16),
        jax.ShapeDtypeStruct((n, LANES), BF16),
        jax.ShapeDtypeStruct((b, GQA_HEADS, t_len, GQA_HEAD_DIM), BF16),
        jax.ShapeDtypeStruct((n, kvw), BF16),
        jax.ShapeDtypeStruct((n, kvw), BF16),
    ]
    out_specs = [heads(MLA_HEADS, MLA_QK_PAD), row(MLA_KV_RANK), row(LANES),
                 heads(GQA_HEADS, GQA_HEAD_DIM), row(kvw), row(kvw)]
    if cache_out:
        out_shape += [
            jax.ShapeDtypeStruct((n, MLA_KV_RANK), F32),
            jax.ShapeDtypeStruct((n, MLA_ROPE), F32),
            jax.ShapeDtypeStruct((n, kvw), F32),
            jax.ShapeDtypeStruct((n, kvw), F32),
        ]
        out_specs += [row(MLA_KV_RANK), row(MLA_ROPE), row(kvw), row(kvw)]
    return pl.pallas_call(
        functools.partial(_qkv_kernel, rope, cache_out),
        grid=(n // tm,),
        in_specs=in_specs,
        out_specs=out_specs,
        out_shape=out_shape,
        compiler_params=_params(("arbitrary",)),
        name="qkv_rope" if rope else "qkv_ctx",
    )(x, modr, modr, lw["g_attn"], lw["w_a"], lw["g_mla_q"], lw["w_uq"], lw["g_mla_kv"],
      lw["g_gqa_q"], lw["g_gqa_k"], *tabs)


def _kv_expand_kernel(ckv_ref, kr_ref, wuk_ref, wuv_ref, k_ref, v_ref):
    ckv = ckv_ref[...]
    kn = _dot(ckv, wuk_ref[...]).astype(BF16)
    kr = kr_ref[...]
    for hd in range(MLA_HEADS):
        k_ref[:, hd * MLA_QK_PAD:hd * MLA_QK_PAD + LANES] = kn[:, hd * MLA_NOPE:(hd + 1) * MLA_NOPE]
        k_ref[:, hd * MLA_QK_PAD + LANES:(hd + 1) * MLA_QK_PAD] = kr
    v_ref[...] = _dot(ckv, wuv_ref[...]).astype(BF16)


def _kv_expand(ckv, kr, w_uk, w_uv):
    b, s, _ = ckv.shape
    ts = min(512, s)
    kw, vw = MLA_HEADS * MLA_QK_PAD, MLA_HEADS * MLA_V
    blk = lambda w: pl.BlockSpec((None, ts, w), lambda i, j: (i, j, 0))
    return pl.pallas_call(
        _kv_expand_kernel,
        grid=(b, s // ts),
        in_specs=[blk(MLA_KV_RANK), blk(LANES), _const_spec((MLA_KV_RANK, MLA_HEADS * MLA_NOPE)),
                  _const_spec((MLA_KV_RANK, vw))],
        out_specs=[blk(kw), blk(vw)],
        out_shape=[jax.ShapeDtypeStruct((b, s, kw), BF16), jax.ShapeDtypeStruct((b, s, vw), BF16)],
        compiler_params=_params(("arbitrary", "arbitrary")),
        name="kv_expand",
    )(ckv, kr, w_uk, w_uv)


def _attn_kernel(tq, tk, dq, hps, rolled, q_ref, k_ref, v_ref, o_ref, s0, s1, m0, m1, vx_ref):
    s_len = k_ref.shape[0]
    rows = q_ref.shape[1]
    n_chunks = s_len // tk
    dv = HEAD_V
    bufs = ((s0, m0), (s1, m1))

    for hh in range(hps):
        vx_ref[hh, :, :dv] = v_ref[:, hh * dv:(hh + 1) * dv]
        vx_ref[hh, :, dv:] = jnp.ones((s_len, dv), BF16)

    def stage(p1, p2):
        if p1 is not None:
            h1, r1, b1 = p1
            q = q_ref[h1, pl.ds(r1, tq), :]
            sb1, mb1 = bufs[b1]
        if p2 is not None:
            h2, r2, b2 = p2
            sb2, mb2 = bufs[b2]
            m = jnp.max(mb2[...], axis=1, keepdims=True)
            acc = None
        for c in range(n_chunks):
            ck = slice(c * tk, (c + 1) * tk)
            if p1 is not None:
                s = _dot_nt(q, k_ref[ck, h1 * dq:(h1 + 1) * dq])
                sb1[:, ck] = s
                mx = s[:, :LANES]
                for t in range(1, tk // LANES):
                    mx = jnp.maximum(mx, s[:, t * LANES:(t + 1) * LANES])
                mb1[...] = mx if c == 0 else jnp.maximum(mb1[...], mx)
            if p2 is not None:
                p = jnp.exp2(sb2[:, ck] - m).astype(BF16)
                part = _dot(p, vx_ref[h2, ck, :])
                acc = part if acc is None else acc + part
        if p2 is not None:
            o_ref[h2, pl.ds(r2, tq), :] = (acc[:, :dv] / acc[:, dv:]).astype(BF16)

    n_units = rows // tq
    if rolled:
        stage((0, 0, 0), None)

        def body(i, carry):
            u = 2 * i
            stage((0, pl.multiple_of((u + 1) * tq, tq), 1), (0, pl.multiple_of(u * tq, tq), 0))
            nxt = jnp.minimum(u + 2, n_units - 1)
            stage((0, pl.multiple_of(nxt * tq, tq), 0), (0, pl.multiple_of((u + 1) * tq, tq), 1))
            return carry

        lax.fori_loop(0, n_units // 2, body, 0)
    else:
        units = [(hh, u * tq) for hh in range(hps) for u in range(n_units)]
        for j in range(len(units) + 1):
            p1 = units[j] + (j % 2,) if j < len(units) else None
            p2 = units[j - 1] + ((j - 1) % 2,) if j > 0 else None
            stage(p1, p2)


def _attn(q, k, v, *, hps, rolled):
    b, hk, rows, dq = q.shape
    s = k.shape[1]
    dv = HEAD_V
    tq = min(ATTN_TQ, rows)
    tk = min(ATTN_TK, s)
    assert rows % tq == 0 and s % tk == 0 and hk % hps == 0
    assert not rolled or (hps == 1 and (rows // tq) % 2 == 0)
    return pl.pallas_call(
        functools.partial(_attn_kernel, tq, tk, dq, hps, rolled),
        grid=(b, hk // hps),
        in_specs=[
            pl.BlockSpec((None, hps, rows, dq), lambda i, h: (i, h, 0, 0)),
            pl.BlockSpec((None, s, hps * dq), lambda i, h: (i, 0, h)),
            pl.BlockSpec((None, s, hps * dv), lambda i, h: (i, 0, h)),
        ],
        out_specs=pl.BlockSpec((None, hps, rows, dv), lambda i, h: (i, h, 0, 0)),
        out_shape=jax.ShapeDtypeStruct((b, hk, rows, dv), BF16),
        scratch_shapes=[pltpu.VMEM((tq, s), F32), pltpu.VMEM((tq, s), F32),
                        pltpu.VMEM((tq, LANES), F32), pltpu.VMEM((tq, LANES), F32),
                        pltpu.VMEM((hps, s, 2 * dv), BF16)],
        compiler_params=_params(("arbitrary", "arbitrary")),
        name="attn_rolled" if rolled else "attn_flat",
    )(q, k, v)


def _out_proj_kernel(x_ref, oa_ref, ob_ref, sh1_ref, sc1_ref, gt1_ref, sh2_ref, sc2_ref, gn1_ref,
                     gn2_ref, wgl_ref, bg_ref, wo_ref, wr_ref, x1_ref, aff_ref):
    d = x_ref.shape[1]
    x = x_ref[...]
    h = (_rms(x) * gn1_ref[...] * (1.0 + sc1_ref[...]) + sh1_ref[...]).astype(BF16)
    ga = 1.0 / (1.0 + jnp.exp(-(_dot(h, wgl_ref[:, :d]) + bg_ref[:, :d])))
    gb = 1.0 / (1.0 + jnp.exp(-(_dot(h, wgl_ref[:, d:]) + bg_ref[:, d:])))
    parts = []
    for hd in range(d // HEAD_V):
        cs = slice(hd * HEAD_V, (hd + 1) * HEAD_V)
        parts.append((ga[:, cs] * oa_ref[hd].astype(F32) + gb[:, cs] * ob_ref[hd].astype(F32)).astype(BF16))
    merged = jnp.concatenate(parts, axis=1)
    x1 = x + gt1_ref[...] * _dot(merged, wo_ref[...])
    x1_ref[...] = x1
    h2 = _rms(x1) * gn2_ref[...] * (1.0 + sc2_ref[...]) + sh2_ref[...]
    h_hi, h_lo = _split(h2)
    n_e = aff_ref.shape[0]
    both = _dot_nt(wr_ref[...], h_hi)
    lg = both[:n_e] + both[n_e:] + _dot_nt(wr_ref[:n_e, :], h_lo)
    e = jnp.exp(lg - jnp.max(lg, axis=0, keepdims=True))
    aff_ref[...] = e / jnp.sum(e, axis=0, keepdims=True)


def _out_proj(x, t_len, oa, ob, modr, mod_row, lw):
    n, d = x.shape
    tm = min(ROW_TM, t_len)
    tpb = t_len // tm
    nh = d // HEAD_V
    row = lambda: pl.BlockSpec((tm, d), lambda i: (i, 0))
    heads = lambda: pl.BlockSpec((None, nh, tm, HEAD_V), lambda i: (i // tpb, 0, i % tpb, 0))
    mod = lambda j: pl.BlockSpec((None, 1, d), lambda i: (mod_row(i * tm) * 6 + j, 0, 0))
    return pl.pallas_call(
        _out_proj_kernel,
        grid=(n // tm,),
        in_specs=[row(), heads(), heads(), mod(0), mod(1), mod(2), mod(3), mod(4),
                  _const_spec((1, d)), _const_spec((1, d)), _const_spec((d, 2 * d)),
                  _const_spec((1, 2 * d)), _const_spec((d, d)),
                  _const_spec((2 * N_EXPERTS, d))],
        out_specs=[row(), pl.BlockSpec((N_EXPERTS, tm), lambda i: (0, i))],
        out_shape=[jax.ShapeDtypeStruct((n, d), F32), jax.ShapeDtypeStruct((N_EXPERTS, n), F32)],
        compiler_params=_params(("arbitrary",)),
        name="out_proj",
    )(x, oa, ob, modr, modr, modr, modr, modr, lw["g_attn"], lw["g_ffn"], lw["w_gl"], lw["b_gate"],
      lw["w_out"], lw["wr"])


def _lane_cumsum(x_bf16, tri):
    t = x_bf16.shape[1]
    blk = tri.shape[0]
    parts = []
    carry = jnp.zeros((x_bf16.shape[0], 1), F32)
    for c in range(t // blk):
        part = _dot(x_bf16[:, c * blk:(c + 1) * blk], tri) + carry
        carry = part[:, blk - 1:blk]
        parts.append(part)
    return parts[0] if len(parts) == 1 else jnp.concatenate(parts, axis=1)


def _router_kernel(cap, aff_ref, idx_ref, w_ref, pos_ref):
    r = pl.program_id(0)
    a = aff_ref[...]
    n_e, t = a.shape
    bits = pltpu.bitcast(a, jnp.int32)

    thr = jnp.zeros((n_e, 1), jnp.int32)
    for b in range(30, -1, -1):
        cand = thr | (1 << b)
        cnt = jnp.sum((bits >= cand).astype(jnp.int32), axis=1, keepdims=True)
        thr = jnp.where(cnt >= cap, cand, thr)
    gt = bits > thr
    eq = bits == thr
    need = (cap - jnp.sum(gt.astype(jnp.int32), axis=1, keepdims=True)).astype(F32)

    blk = min(MXU_DIM, t)
    ri = lax.broadcasted_iota(jnp.int32, (blk, blk), 0)
    ci = lax.broadcasted_iota(jnp.int32, (blk, blk), 1)
    tri = (ri <= ci).astype(BF16)
    cum_eq = _lane_cumsum(eq.astype(BF16), tri)
    sel = gt | (eq & (cum_eq <= need))
    cnt_incl = _lane_cumsum(sel.astype(BF16), tri)
    w_ref[...] = jnp.where(sel, a, 0.0)
    pos_ref[...] = (cnt_incl - sel.astype(F32)).astype(jnp.int32) + r * cap

    ts = min(LANES, cap)
    ones = jnp.ones((t, LANES), BF16)
    lane = lax.broadcasted_iota(jnp.int32, (1, LANES), 1)
    for sb in range(cap // ts):
        s_iota = (lax.broadcasted_iota(jnp.int32, (ts, 1), 0) + sb * ts).astype(F32)
        out = jnp.zeros((ts, LANES), F32)
        for e in range(n_e):
            cmp = (cnt_incl[e:e + 1, :] <= s_iota).astype(BF16)
            out = out + jnp.where(lane == e, _dot(cmp, ones), 0.0)
        idx_ref[sb * ts:(sb + 1) * ts, :] = out.astype(jnp.int32) + r * t


def _router(aff_t, n_req):
    n_e, n = aff_t.shape
    t = n // n_req
    cap = EC_FACTOR * t // N_EXPERTS
    return pl.pallas_call(
        functools.partial(_router_kernel, cap),
        grid=(n_req,),
        in_specs=[pl.BlockSpec((n_e, t), lambda r: (0, r))],
        out_specs=[pl.BlockSpec((cap, LANES), lambda r: (r, 0)),
                   pl.BlockSpec((n_e, t), lambda r: (0, r)),
                   pl.BlockSpec((n_e, t), lambda r: (0, r))],
        out_shape=[jax.ShapeDtypeStruct((n_req * cap, LANES), jnp.int32),
                   jax.ShapeDtypeStruct((n_e, n), F32),
                   jax.ShapeDtypeStruct((n_e, n), jnp.int32)],
        compiler_params=_params(("arbitrary",)),
        name="router",
    )(aff_t)


def _ffn_kernel(ts, idx_ref, xp_hbm, xs_hbm, sh_ref, sc_ref, gn_ref, wg_ref, wu_ref, wd_ref, ye_ref,
                xbuf, sem):
    n_tiles = pl.num_programs(1)
    step = pl.program_id(0) * n_tiles + pl.program_id(1)
    last = pl.num_programs(0) * n_tiles - 1
    slot = step % 2
    nslot = 1 - slot
    d = xbuf.shape[2]

    def row_copy(src, tile, s, buf):
        return pltpu.make_async_copy(src.at[pl.ds(idx_ref[tile * ts + s], 1), :],
                                     xbuf.at[buf, pl.ds(s, 1), :], sem.at[buf])

    def wait_tile(buf):
        pltpu.make_async_copy(xs_hbm.at[pl.ds(0, ts), :], xbuf.at[buf], sem.at[buf]).wait()

    @pl.when(step == 0)
    def _():
        def body(s, carry):
            row_copy(xp_hbm, 0, s, 0).start()
            return carry
        lax.fori_loop(0, ts, body, 0, unroll=8)

    wait_tile(slot)
    nxt = jnp.minimum(step + 1, last)
    n_groups = FFN_GATE_CHUNKS + FFN_DOWN_CHUNKS
    per_group = ts // n_groups

    def run(src):
        issued = [0]

        def issue_group():
            for s in range(issued[0], issued[0] + per_group):
                row_copy(src, nxt, s, nslot).start()
            issued[0] += per_group

        h = (_rms(xbuf[slot]) * gn_ref[...] * (1.0 + sc_ref[...]) + sh_ref[...]).astype(BF16)
        fw = D_FF // FFN_GATE_CHUNKS
        hid = []
        for c in range(FFN_GATE_CHUNKS):
            issue_group()
            gate = _dot(h, wg_ref[:, c * fw:(c + 1) * fw])
            up = _dot(h, wu_ref[:, c * fw:(c + 1) * fw])
            hid.append((gate / (1.0 + jnp.exp(-gate)) * up).astype(BF16))
        hid = jnp.concatenate(hid, axis=1)
        dw = d // FFN_DOWN_CHUNKS
        for c in range(FFN_DOWN_CHUNKS):
            issue_group()
            ye_ref[:, c * dw:(c + 1) * dw] = _dot(hid, wd_ref[:, c * dw:(c + 1) * dw]).astype(BF16)

    next_is_context = nxt % n_tiles == 0

    @pl.when(next_is_context)
    def _():
        run(xp_hbm)

    @pl.when(jnp.logical_not(next_is_context))
    def _():
        run(xs_hbm)

    @pl.when(step == last)
    def _():
        wait_tile(nslot)


def _ffn(idx, x1p, x1s, modr, layer, lw, n_tiles, ts):
    d = x1p.shape[1]
    assert ts % (FFN_GATE_CHUNKS + FFN_DOWN_CHUNKS) == 0
    assert D_FF % FFN_GATE_CHUNKS == 0 and d % FFN_DOWN_CHUNKS == 0
    mod = lambda c: pl.BlockSpec((None, 1, d), lambda e, j, *_: ((layer * 8 + j) * 6 + c, 0, 0))
    grid_spec = pltpu.PrefetchScalarGridSpec(
        num_scalar_prefetch=1,
        grid=(N_EXPERTS, n_tiles),
        in_specs=[
            pl.BlockSpec(memory_space=pl.ANY), pl.BlockSpec(memory_space=pl.ANY), mod(3), mod(4),
            pl.BlockSpec((1, d), lambda e, j, *_: (0, 0)),
            pl.BlockSpec((None, d, D_FF), lambda e, j, *_: (e, 0, 0)),
            pl.BlockSpec((None, d, D_FF), lambda e, j, *_: (e, 0, 0)),
            pl.BlockSpec((None, D_FF, d), lambda e, j, *_: (e, 0, 0)),
        ],
        out_specs=pl.BlockSpec((None, ts, d), lambda e, j, *_: (e, j, 0)),
        scratch_shapes=[pltpu.VMEM((2, ts, d), F32), pltpu.SemaphoreType.DMA((2,))],
    )
    return pl.pallas_call(
        functools.partial(_ffn_kernel, ts),
        grid_spec=grid_spec,
        out_shape=jax.ShapeDtypeStruct((N_EXPERTS, n_tiles * ts, d), BF16),
        compiler_params=_params(("arbitrary", "arbitrary")),
        name="ffn",
    )(idx, x1p, x1s, modr, modr, lw["g_ffn"], lw["w_gate"], lw["w_up"], lw["w_down"])


def _combine_kernel(final, tm, slots, starts_ref, ye_hbm, x_ref, pos_ref, w_ref, gt_ref, gf_ref, o_ref,
                    stage, stage_x, acc_ref, sem, sem_x):
    i = pl.program_id(0)
    last = pl.num_programs(0) - 1
    n_e, wn = N_EXPERTS, COMBINE_W
    per_pass = MXU_DIM // wn
    top = n_e * slots - wn
    slot = i % 2
    nslot = 1 - slot

    def window(tile, e, c):
        s_lo = starts_ref[tile * n_e + e]
        nominal = e * slots + (s_lo // BF16_ROWS) * BF16_ROWS + c * wn
        return jnp.minimum(nominal, top), nominal

    def win_copy(tile, e, c, dst, s):
        base, _ = window(tile, e, c)
        return pltpu.make_async_copy(ye_hbm.at[pl.ds(pl.multiple_of(base, BF16_ROWS), wn), :],
                                     dst.at[pl.ds(e * wn, wn), :], s)

    @pl.when(i == 0)
    def _():
        for e in range(n_e):
            win_copy(0, e, 0, stage.at[0], sem.at[0]).start()

    pltpu.make_async_copy(ye_hbm.at[pl.ds(0, n_e * wn), :], stage.at[slot], sem.at[slot]).wait()
    nxt = jnp.minimum(i + 1, last)
    for e in range(n_e):
        win_copy(nxt, e, 0, stage.at[nslot], sem.at[nslot]).start()

    pos = pos_ref[...]
    w = w_ref[...]
    lane = lax.broadcasted_iota(jnp.int32, (1, MXU_DIM), 1)

    def accumulate(src, c, first):
        for g in range(n_e // per_pass):
            tgt = val = None
            for j in range(per_pass):
                e = g * per_pass + j
                base, nominal = window(i, e, c)
                gs = pos[:, e:e + 1] + e * slots
                t_j = gs - (base - j * wn)
                in_round = (gs >= nominal) & (gs < nominal + wn)
                v_j = jnp.where(in_round, w[:, e:e + 1], 0.0)
                if j == 0:
                    tgt, val = t_j, v_j
                else:
                    here = lane >= j * wn
                    tgt = jnp.where(here, t_j, tgt)
                    val = jnp.where(here, v_j, val)
            place = jnp.where(tgt == lane, val, 0.0).astype(BF16)
            part = _dot(place, src[g * MXU_DIM:(g + 1) * MXU_DIM, :])
            if first and g == 0:
                acc_ref[...] = part
            else:
                acc_ref[...] += part

    accumulate(stage.at[slot], 0, True)

    rounds = 1
    for e in range(n_e):
        s_lo = starts_ref[i * n_e + e]
        span = starts_ref[(i + 1) * n_e + e] - (s_lo // BF16_ROWS) * BF16_ROWS
        rounds = jnp.maximum(rounds, (span + wn - 1) // wn)

    def extra_round(c, carry):
        for e in range(n_e):
            win_copy(i, e, c, stage_x, sem_x).start()
        pltpu.make_async_copy(ye_hbm.at[pl.ds(0, n_e * wn), :], stage_x, sem_x).wait()
        accumulate(stage_x, c, False)
        return carry

    lax.fori_loop(1, rounds, extra_round, 0)

    y = x_ref[...] + gt_ref[...] * acc_ref[...]
    if final:
        y = _rms(y) * gf_ref[...]
    o_ref[...] = y

    @pl.when(i == last)
    def _():
        pltpu.make_async_copy(ye_hbm.at[pl.ds(0, n_e * wn), :], stage.at[nslot], sem.at[nslot]).wait()


def _combine(starts, ye, x1, pos, w, modr, mod_row, g_final, *, slots, final):
    n, d = x1.shape
    tm = COMBINE_TM
    assert MXU_DIM % COMBINE_W == 0 and N_EXPERTS % (MXU_DIM // COMBINE_W) == 0
    grid_spec = pltpu.PrefetchScalarGridSpec(
        num_scalar_prefetch=1,
        grid=(n // tm,),
        in_specs=[
            pl.BlockSpec(memory_space=pl.ANY),
            pl.BlockSpec((tm, d), lambda i, *_: (i, 0)),
            pl.BlockSpec((tm, LANES), lambda i, *_: (i, 0)),
            pl.BlockSpec((tm, LANES), lambda i, *_: (i, 0)),
            pl.BlockSpec((None, 1, d), lambda i, *_: (mod_row(i * tm) * 6 + 5, 0, 0)),
            pl.BlockSpec((1, d), lambda i, *_: (0, 0)),
        ],
        out_specs=pl.BlockSpec((tm, d), lambda i, *_: (i, 0)),
        scratch_shapes=[pltpu.VMEM((2, N_EXPERTS * COMBINE_W, d), BF16),
                        pltpu.VMEM((N_EXPERTS * COMBINE_W, d), BF16),
                        pltpu.VMEM((tm, d), F32),
                        pltpu.SemaphoreType.DMA((2,)), pltpu.SemaphoreType.DMA(())],
    )
    return pl.pallas_call(
        functools.partial(_combine_kernel, final, tm, slots),
        grid_spec=grid_spec,
        out_shape=jax.ShapeDtypeStruct((n, d), F32),
        compiler_params=_params(("arbitrary",)),
        name="combine_final" if final else "combine",
    )(starts, ye, x1, pos, w, modr, g_final)


def _rope_tables(n_tokens, rot_dim):
    t = jnp.arange(n_tokens)
    row = (t // GRID_W).astype(F32)
    col = (t % GRID_W).astype(F32)
    quarter = rot_dim // 4
    inv = ROPE_THETA ** (-jnp.arange(quarter, dtype=F32) / quarter)
    ar, ac = row[:, None] * inv, col[:, None] * inv
    cos = jnp.concatenate([jnp.cos(ar), jnp.cos(ar), jnp.cos(ac), jnp.cos(ac)], axis=-1)
    sin = jnp.concatenate([-jnp.sin(ar), jnp.sin(ar), -jnp.sin(ac), jnp.sin(ac)], axis=-1)
    pad = LANES - rot_dim
    if pad:
        cos = jnp.concatenate([cos, jnp.ones((n_tokens, pad), F32)], axis=-1)
        sin = jnp.concatenate([sin, jnp.zeros((n_tokens, pad), F32)], axis=-1)
    return cos, sin


def _layer_weights(l, g_attn_norm, w_in, b_gate, g_mla_q, w_mla_uq, g_mla_kv, w_mla_uk, w_mla_uv,
                   g_gqa_q, g_gqa_k, w_out, g_ffn_norm, w_router, w_gate_e, w_up_e, w_down_e):
    d = D_MODEL
    o_ckv = MLA_Q_RANK + MLA_KV_RANK
    o_kr = o_ckv + MLA_ROPE
    o_gl = o_kr + (GQA_HEADS + 2 * GQA_KV_HEADS) * GQA_HEAD_DIM
    w = w_in[l]
    w_a = jnp.concatenate(
        [w[:, :o_kr], jnp.zeros((d, LANES - MLA_ROPE), F32), w[:, o_kr:o_gl]], axis=1).astype(BF16)
    w_uq = w_mla_uq[l].reshape(MLA_Q_RANK, MLA_HEADS, MLA_QK)
    w_uq = jnp.pad(w_uq, ((0, 0), (0, 0), (0, MLA_QK_PAD - MLA_QK))).reshape(MLA_Q_RANK, -1).astype(BF16)
    wr = w_router[l].T
    wr_hi = wr.astype(BF16)
    wr_lo = (wr - wr_hi.astype(F32)).astype(BF16)
    return {
        "g_attn": g_attn_norm[l][None], "g_ffn": g_ffn_norm[l][None], "w_a": w_a,
        "g_mla_q": g_mla_q[l][None], "w_uq": w_uq, "g_mla_kv": g_mla_kv[l][None],
        "g_gqa_q": g_gqa_q[l][None], "g_gqa_k": g_gqa_k[l][None],
        "w_uk": w_mla_uk[l].astype(BF16), "w_uv": w_mla_uv[l].astype(BF16),
        "w_gl": w[:, o_gl:].astype(BF16), "b_gate": b_gate[l][None], "w_out": w_out[l].astype(BF16),
        "wr": jnp.concatenate([wr_hi, wr_lo], axis=0),
        "w_gate": w_gate_e[l].astype(BF16), "w_up": w_up_e[l].astype(BF16),
        "w_down": w_down_e[l].astype(BF16),
    }


def _combine_tables(pos, w, slot_base, n_slots):
    pos = pos + slot_base
    starts = jnp.concatenate([pos[:, ::COMBINE_TM].T,
                              jnp.full((1, N_EXPERTS), slot_base + n_slots, jnp.int32)], axis=0)
    lanes = lambda a: jnp.pad(a.T, ((0, 0), (0, LANES - N_EXPERTS)))
    return starts.reshape(-1), lanes(pos), lanes(w)


def kernel(x_prompt, x_sample, c, cache_mla_ckv, cache_mla_krope, cache_gqa_k, cache_gqa_v, c_ctx,
           w_ada, b_ada, g_attn_norm, w_in, b_gate, g_mla_q, w_mla_uq, g_mla_kv, w_mla_uk, w_mla_uv,
           g_gqa_q, g_gqa_k, w_out, g_ffn_norm, w_router, w_gate_e, w_up_e, w_down_e, g_final):
    bp, tp, d = x_prompt.shape
    bs, tl, _ = x_sample.shape
    n_layers = w_ada.shape[0]
    np_, ns = bp * tp, bs * tl
    kvw = GQA_KV_HEADS * GQA_HEAD_DIM
    cap_p = EC_FACTOR * tp // N_EXPERTS
    cap_s = EC_FACTOR * tl // N_EXPERTS
    ts = cap_s
    assert bp * cap_p == ts and bs + 1 <= 8
    n_tiles = 1 + bs
    slots = n_tiles * ts

    cc = jnp.zeros((8, d), F32).at[0].set(c_ctx).at[1:1 + bs].set(c)
    modr = _ada(cc, w_ada, b_ada).reshape(n_layers * 8 * 6, 1, d)

    cos_m, sin_m = _rope_tables(tl, MLA_ROPE)
    cos_g, sin_g = _rope_tables(tl, GQA_HEAD_DIM)
    tabs = (cos_m, sin_m, cos_g, sin_g)
    g_fin = g_final[None]

    xp = x_prompt.reshape(np_, d)
    xs = x_sample.reshape(ns, d)
    st = {"ckv": [], "kr": [], "k": [], "v": []}
    for l in range(n_layers):
        lw = _layer_weights(l, g_attn_norm, w_in, b_gate, g_mla_q, w_mla_uq, g_mla_kv, w_mla_uk,
                            w_mla_uv, g_gqa_q, g_gqa_k, w_out, g_ffn_norm, w_router, w_gate_e,
                            w_up_e, w_down_e)
        row_p = lambda r, l=l: l * 8
        row_s = lambda r, l=l: l * 8 + 1 + r // tl

        qm, ckv, kr, qg, kg, vg, ckv32, kr32, kg32, vg32 = _qkv(
            xp, tp, modr, row_p, lw, tabs, rope=False, cache_out=True)
        st["ckv"].append(ckv32.reshape(bp, tp, MLA_KV_RANK))
        st["kr"].append(kr32.reshape(bp, tp, MLA_ROPE))
        st["k"].append(kg32.reshape(bp, tp, GQA_KV_HEADS, GQA_HEAD_DIM))
        st["v"].append(vg32.reshape(bp, tp, GQA_KV_HEADS, GQA_HEAD_DIM))
        k_m, v_m = _kv_expand(ckv.reshape(bp, tp, -1), kr.reshape(bp, tp, -1), lw["w_uk"], lw["w_uv"])
        oa_p = _attn(qm, k_m, v_m, hps=MLA_HEADS, rolled=False)
        ob_p = _attn(qg.reshape(bp, GQA_KV_HEADS, GQA_GROUP * tp, GQA_HEAD_DIM), kg.reshape(bp, tp, -1),
                     vg.reshape(bp, tp, -1), hps=GQA_KV_HEADS, rolled=False)
        x1p, aff_p = _out_proj(xp, tp, oa_p, ob_p.reshape(bp, GQA_HEADS, tp, HEAD_V), modr, row_p, lw)

        qm, ckv, kr, qg, kg, vg = _qkv(xs, tl, modr, row_s, lw, tabs, rope=True, cache_out=False)
        kr_cache = jnp.pad(cache_mla_krope[:, l], ((0, 0), (0, 0), (0, LANES - MLA_ROPE))).astype(BF16)
        ckv_all = jnp.concatenate([cache_mla_ckv[:, l].astype(BF16), ckv.reshape(bs, tl, -1)], axis=1)
        kr_all = jnp.concatenate([kr_cache, kr.reshape(bs, tl, -1)], axis=1)
        k_m, v_m = _kv_expand(ckv_all, kr_all, lw["w_uk"], lw["w_uv"])
        oa_s = _attn(qm, k_m, v_m, hps=1, rolled=True)
        past = cache_gqa_k.shape[2]
        k_all = jnp.concatenate([cache_gqa_k[:, l].reshape(bs, past, kvw).astype(BF16),
                                 kg.reshape(bs, tl, kvw)], axis=1)
        v_all = jnp.concatenate([cache_gqa_v[:, l].reshape(bs, past, kvw).astype(BF16),
                                 vg.reshape(bs, tl, kvw)], axis=1)
        ob_s = _attn(qg.reshape(bs, GQA_KV_HEADS, GQA_GROUP * tl, GQA_HEAD_DIM), k_all, v_all,
                     hps=1, rolled=True)
        x1s, aff_s = _out_proj(xs, tl, oa_s, ob_s.reshape(bs, GQA_HEADS, tl, HEAD_V), modr, row_s, lw)

        idx_p, w_p, pos_p = _router(aff_p, bp)
        idx_s, w_s, pos_s = _router(aff_s, bs)
        idx = jnp.concatenate([idx_p[:, :N_EXPERTS], idx_s[:, :N_EXPERTS]], axis=0).T.reshape(-1)
        ye = _ffn(idx, x1p, x1s, modr, l, lw, n_tiles, ts).reshape(N_EXPERTS * slots, d)
        final = l == n_layers - 1
        st_p, ps_p, wt_p = _combine_tables(pos_p, w_p, 0, bp * cap_p)
        st_s, ps_s, wt_s = _combine_tables(pos_s, w_s, ts, bs * cap_s)
        xp = _combine(st_p, ye, x1p, ps_p, wt_p, modr, row_p, g_fin, slots=slots, final=final)
        xs = _combine(st_s, ye, x1s, ps_s, wt_s, modr, row_s, g_fin, slots=slots, final=final)

    return (xp.reshape(bp, tp, d), xs.reshape(bs, tl, d),
            jnp.stack(st["ckv"], axis=1), jnp.stack(st["kr"], axis=1),
            jnp.stack(st["k"], axis=1), jnp.stack(st["v"], axis=1))
```

```python
import functools

import jax
import jax.numpy as jnp
from jax import lax
from jax.experimental import pallas as pl
from jax.experimental.pallas import tpu as pltpu

F32 = jnp.float32
BF16 = jnp.bfloat16

D_MODEL = 2048
GRID_W = 64
ROPE_THETA = 10000.0
EPS = 1e-6
MLA_HEADS = 16
MLA_Q_RANK = 512
MLA_KV_RANK = 256
MLA_NOPE = 128
MLA_ROPE = 64
MLA_V = 128
MLA_QK = MLA_NOPE + MLA_ROPE
MLA_SCALE = MLA_QK ** -0.5
GQA_HEADS = 16
GQA_KV_HEADS = 4
GQA_GROUP = GQA_HEADS // GQA_KV_HEADS
GQA_HEAD_DIM = 128
GQA_SCALE = GQA_HEAD_DIM ** -0.5
HEAD_V = 128
N_EXPERTS = 16
EC_FACTOR = 2
D_FF = 1024
LOG2E = 1.4426950408889634

LANES = 128
BF16_ROWS = 16
MXU_DIM = 256
MLA_QK_PAD = MXU_DIM
VMEM_LIMIT = 56 * 1024 * 1024

ROW_TM = 256
ATTN_TQ = 512
ATTN_TK = 512
COMBINE_TM = 256
COMBINE_W = 64
FFN_GATE_CHUNKS = 4
FFN_DOWN_CHUNKS = 4

C_CQ = 0
C_CKV = C_CQ + MLA_Q_RANK
C_KR = C_CKV + MLA_KV_RANK
C_GQ = C_KR + LANES
C_GK = C_GQ + GQA_HEADS * GQA_HEAD_DIM
C_GV = C_GK + GQA_KV_HEADS * GQA_HEAD_DIM
C_END = C_GV + GQA_KV_HEADS * GQA_HEAD_DIM


def _dot(a, b):
    return jnp.dot(a, b, preferred_element_type=F32)


def _dot_nt(a, b):
    return lax.dot_general(a, b, (((1,), (1,)), ((), ())), preferred_element_type=F32)


def _split(x):
    hi = x.astype(BF16)
    lo = (x - hi.astype(F32)).astype(BF16)
    return hi, lo


def _rms(x):
    return x * lax.rsqrt(jnp.mean(x * x, axis=-1, keepdims=True) + EPS)


def _params(sem):
    return pltpu.CompilerParams(dimension_semantics=sem, vmem_limit_bytes=VMEM_LIMIT)


def _const_spec(shape):
    nd = len(shape)
    return pl.BlockSpec(shape, lambda *_: (0,) * nd, pipeline_mode=pl.Buffered(1))


def _ada_kernel(c_ref, w_ref, b_ref, o_ref):
    c = c_ref[...]
    s = c / (1.0 + jnp.exp(-c))
    rows = c.shape[0]
    s_hi, s_lo = _split(s)
    w_hi, w_lo = _split(w_ref[...])
    both = _dot(jnp.concatenate([s_hi, s_lo], axis=0), w_hi)
    o_ref[...] = both[:rows] + both[rows:] + _dot(s_hi, w_lo) + b_ref[...]


def _ada(cc, w_ada, b_ada):
    n_layers, d, n6 = w_ada.shape
    rows = cc.shape[0]
    tn = 1024
    return pl.pallas_call(
        _ada_kernel,
        grid=(n_layers, n6 // tn),
        in_specs=[
            pl.BlockSpec((rows, d), lambda l, j: (0, 0)),
            pl.BlockSpec((None, d, tn), lambda l, j: (l, 0, j)),
            pl.BlockSpec((None, 1, tn), lambda l, j: (l, 0, j)),
        ],
        out_specs=pl.BlockSpec((None, rows, tn), lambda l, j: (l, 0, j)),
        out_shape=jax.ShapeDtypeStruct((n_layers, rows, n6), F32),
        compiler_params=_params(("arbitrary", "arbitrary")),
        name="ada",
    )(cc, w_ada, b_ada.reshape(n_layers, 1, n6))


def _swap_halves(x, q):
    lane = lax.broadcasted_iota(jnp.int32, x.shape, 1)
    first = (lane % (2 * q)) < q
    return jnp.where(first, pltpu.roll(x, LANES - q, 1), pltpu.roll(x, q, 1))


def _qkv_kernel(rope, cache_out, x_ref, sh_ref, sc_ref, gn_ref, w_ref, gq_ref, wuq_ref, gkv_ref,
                ggq_ref, ggk_ref, cm_ref, sm_ref, cg_ref, sg_ref, *outs):
    qm_ref, ckv_ref, kr_ref, qg_ref, kg_ref, vg_ref = outs[:6]
    x = x_ref[...]
    h = (_rms(x) * gn_ref[...] * (1.0 + sc_ref[...]) + sh_ref[...]).astype(BF16)

    cq = _rms(_dot(h, w_ref[:, C_CQ:C_CKV])) * gq_ref[...]
    qm = _dot(cq.astype(BF16), wuq_ref[...]) * (MLA_SCALE * LOG2E)
    for hd in range(MLA_HEADS):
        lo = hd * MLA_QK_PAD
        qm_ref[hd, :, :LANES] = qm[:, lo:lo + LANES].astype(BF16)
        t = qm[:, lo + LANES:lo + 2 * LANES]
        if rope:
            t = t * cm_ref[...] + _swap_halves(t, MLA_ROPE // 4) * sm_ref[...]
        qm_ref[hd, :, LANES:] = t.astype(BF16)

    ckv = _rms(_dot(h, w_ref[:, C_CKV:C_KR])) * gkv_ref[...]
    ckv_ref[...] = ckv.astype(BF16)
    kr = _dot(h, w_ref[:, C_KR:C_GQ])
    if cache_out:
        outs[6][...] = ckv
        outs[7][...] = kr[:, :MLA_ROPE]
    if rope:
        kr = kr * cm_ref[...] + _swap_halves(kr, MLA_ROPE // 4) * sm_ref[...]
    kr_ref[...] = kr.astype(BF16)

    zq = _dot(h, w_ref[:, C_GQ:C_GK])
    for hd in range(GQA_HEADS):
        lo = hd * GQA_HEAD_DIM
        t = _rms(zq[:, lo:lo + GQA_HEAD_DIM]) * ggq_ref[...]
        if rope:
            t = t * cg_ref[...] + _swap_halves(t, GQA_HEAD_DIM // 4) * sg_ref[...]
        qg_ref[hd] = (t * (GQA_SCALE * LOG2E)).astype(BF16)
    zk = _dot(h, w_ref[:, C_GK:C_GV])
    for hd in range(GQA_KV_HEADS):
        lo = hd * GQA_HEAD_DIM
        t = _rms(zk[:, lo:lo + GQA_HEAD_DIM]) * ggk_ref[...]
        if cache_out:
            outs[8][:, lo:lo + GQA_HEAD_DIM] = t
        if rope:
            t = t * cg_ref[...] + _swap_halves(t, GQA_HEAD_DIM // 4) * sg_ref[...]
        kg_ref[:, lo:lo + GQA_HEAD_DIM] = t.astype(BF16)
    zv = _dot(h, w_ref[:, C_GV:C_END])
    vg_ref[...] = zv.astype(BF16)
    if cache_out:
        outs[9][...] = zv


def _qkv(x, t_len, modr, mod_row, lw, tabs, *, rope, cache_out):
    n, d = x.shape
    b = n // t_len
    tm = min(ROW_TM, t_len)
    tpb = t_len // tm
    pos_tiles = tabs[0].shape[0] // tm
    kvw = GQA_KV_HEADS * GQA_HEAD_DIM
    row = lambda w: pl.BlockSpec((tm, w), lambda i: (i, 0))
    row_in = row(d)
    heads = lambda nh, w: pl.BlockSpec((None, nh, tm, w), lambda i: (i // tpb, 0, i % tpb, 0))
    mod = lambda j: pl.BlockSpec((None, 1, d), lambda i: (mod_row(i * tm) * 6 + j, 0, 0))
    tab = lambda w: pl.BlockSpec((tm, w), lambda i: (i % pos_tiles, 0))
    in_specs = [
        row_in, mod(0), mod(1), _const_spec((1, d)), _const_spec((d, C_END)),
        _const_spec((1, MLA_Q_RANK)), _const_spec((MLA_Q_RANK, MLA_HEADS * MLA_QK_PAD)),
        _const_spec((1, MLA_KV_RANK)), _const_spec((1, GQA_HEAD_DIM)), _const_spec((1, GQA_HEAD_DIM)),
        tab(LANES), tab(LANES), tab(LANES), tab(LANES),
    ]
    out_shape = [
        jax.ShapeDtypeStruct((b, MLA_HEADS, t_len, MLA_QK_PAD), BF16),
        jax.ShapeDtypeStruct((n, MLA_KV_RANK), BF16),
        jax.ShapeDtypeStruct((n, LANES), BF16),
        jax.ShapeDtypeStruct((b, GQA_HEADS, t_len, GQA_HEAD_DIM), BF16),
        jax.ShapeDtypeStruct((n, kvw), BF16),
        jax.ShapeDtypeStruct((n, kvw), BF16),
    ]
    out_specs = [heads(MLA_HEADS, MLA_QK_PAD), row(MLA_KV_RANK), row(LANES),
                 heads(GQA_HEADS, GQA_HEAD_DIM), row(kvw), row(kvw)]
    if cache_out:
        out_shape += [
            jax.ShapeDtypeStruct((n, MLA_KV_RANK), F32),
            jax.ShapeDtypeStruct((n, MLA_ROPE), F32),
            jax.ShapeDtypeStruct((n, kvw), F32),
            jax.ShapeDtypeStruct((n, kvw), F32),
        ]
        out_specs += [row(MLA_KV_RANK), row(MLA_ROPE), row(kvw), row(kvw)]
    return pl.pallas_call(
        functools.partial(_qkv_kernel, rope, cache_out),
        grid=(n // tm,),
        in_specs=in_specs,
        out_specs=out_specs,
        out_shape=out_shape,
        compiler_params=_params(("arbitrary",)),
        name="qkv_rope" if rope else "qkv_ctx",
    )(x, modr, modr, lw["g_attn"], lw["w_a"], lw["g_mla_q"], lw["w_uq"], lw["g_mla_kv"],
      lw["g_gqa_q"], lw["g_gqa_k"], *tabs)


def _kv_expand_kernel(ckv_ref, kr_ref, wuk_ref, wuv_ref, k_ref, v_ref):
    ckv = ckv_ref[...]
    kn = _dot(ckv, wuk_ref[...]).astype(BF16)
    kr = kr_ref[...]
    for hd in range(MLA_HEADS):
        k_ref[:, hd * MLA_QK_PAD:hd * MLA_QK_PAD + LANES] = kn[:, hd * MLA_NOPE:(hd + 1) * MLA_NOPE]
        k_ref[:, hd * MLA_QK_PAD + LANES:(hd + 1) * MLA_QK_PAD] = kr
    v_ref[...] = _dot(ckv, wuv_ref[...]).astype(BF16)


def _kv_expand(ckv, kr, w_uk, w_uv):
    b, s, _ = ckv.shape
    ts = min(512, s)
    kw, vw = MLA_HEADS * MLA_QK_PAD, MLA_HEADS * MLA_V
    blk = lambda w: pl.BlockSpec((None, ts, w), lambda i, j: (i, j, 0))
    return pl.pallas_call(
        _kv_expand_kernel,
        grid=(b, s // ts),
        in_specs=[blk(MLA_KV_RANK), blk(LANES), _const_spec((MLA_KV_RANK, MLA_HEADS * MLA_NOPE)),
                  _const_spec((MLA_KV_RANK, vw))],
        out_specs=[blk(kw), blk(vw)],
        out_shape=[jax.ShapeDtypeStruct((b, s, kw), BF16), jax.ShapeDtypeStruct((b, s, vw), BF16)],
        compiler_params=_params(("arbitrary", "arbitrary")),
        name="kv_expand",
    )(ckv, kr, w_uk, w_uv)


def _attn_kernel(tq, tk, dq, hps, rolled, q_ref, k_ref, v_ref, o_ref, s0, s1, m0, m1, vx_ref):
    s_len = k_ref.shape[0]
    rows = q_ref.shape[1]
    n_chunks = s_len // tk
    dv = HEAD_V
    bufs = ((s0, m0), (s1, m1))

    for hh in range(hps):
        vx_ref[hh, :, :dv] = v_ref[:, hh * dv:(hh + 1) * dv]
        vx_ref[hh, :, dv:] = jnp.ones((s_len, dv), BF16)

    def stage(p1, p2):
        if p1 is not None:
            h1, r1, b1 = p1
            q = q_ref[h1, pl.ds(r1, tq), :]
            sb1, mb1 = bufs[b1]
        if p2 is not None:
            h2, r2, b2 = p2
            sb2, mb2 = bufs[b2]
            m = jnp.max(mb2[...], axis=1, keepdims=True)
            acc = None
        for c in range(n_chunks):
            ck = slice(c * tk, (c + 1) * tk)
            if p1 is not None:
                s = _dot_nt(q, k_ref[ck, h1 * dq:(h1 + 1) * dq])
                sb1[:, ck] = s
                mx = s[:, :LANES]
                for t in range(1, tk // LANES):
                    mx = jnp.maximum(mx, s[:, t * LANES:(t + 1) * LANES])
                mb1[...] = mx if c == 0 else jnp.maximum(mb1[...], mx)
            if p2 is not None:
                p = jnp.exp2(sb2[:, ck] - m).astype(BF16)
                part = _dot(p, vx_ref[h2, ck, :])
                acc = part if acc is None else acc + part
        if p2 is not None:
            o_ref[h2, pl.ds(r2, tq), :] = (acc[:, :dv] / acc[:, dv:]).astype(BF16)

    n_units = rows // tq
    if rolled:
        stage((0, 0, 0), None)

        def body(i, carry):
            u = 2 * i
            stage((0, pl.multiple_of((u + 1) * tq, tq), 1), (0, pl.multiple_of(u * tq, tq), 0))
            nxt = jnp.minimum(u + 2, n_units - 1)
            stage((0, pl.multiple_of(nxt * tq, tq), 0), (0, pl.multiple_of((u + 1) * tq, tq), 1))
            return carry

        lax.fori_loop(0, n_units // 2, body, 0)
    else:
        units = [(hh, u * tq) for hh in range(hps) for u in range(n_units)]
        for j in range(len(units) + 1):
            p1 = units[j] + (j % 2,) if j < len(units) else None
            p2 = units[j - 1] + ((j - 1) % 2,) if j > 0 else None
            stage(p1, p2)


def _attn(q, k, v, *, hps, rolled):
    b, hk, rows, dq = q.shape
    s = k.shape[1]
    dv = HEAD_V
    tq = min(ATTN_TQ, rows)
    tk = min(ATTN_TK, s)
    assert rows % tq == 0 and s % tk == 0 and hk % hps == 0
    assert not rolled or (hps == 1 and (rows // tq) % 2 == 0)
    return pl.pallas_call(
        functools.partial(_attn_kernel, tq, tk, dq, hps, rolled),
        grid=(b, hk // hps),
        in_specs=[
            pl.BlockSpec((None, hps, rows, dq), lambda i, h: (i, h, 0, 0)),
            pl.BlockSpec((None, s, hps * dq), lambda i, h: (i, 0, h)),
            pl.BlockSpec((None, s, hps * dv), lambda i, h: (i, 0, h)),
        ],
        out_specs=pl.BlockSpec((None, hps, rows, dv), lambda i, h: (i, h, 0, 0)),
        out_shape=jax.ShapeDtypeStruct((b, hk, rows, dv), BF16),
        scratch_shapes=[pltpu.VMEM((tq, s), F32), pltpu.VMEM((tq, s), F32),
                        pltpu.VMEM((tq, LANES), F32), pltpu.VMEM((tq, LANES), F32),
                        pltpu.VMEM((hps, s, 2 * dv), BF16)],
        compiler_params=_params(("arbitrary", "arbitrary")),
        name="attn_rolled" if rolled else "attn_flat",
    )(q, k, v)


def _out_proj_kernel(x_ref, oa_ref, ob_ref, sh1_ref, sc1_ref, gt1_ref, sh2_ref, sc2_ref, gn1_ref,
                     gn2_ref, wgl_ref, bg_ref, wo_ref, wr_ref, x1_ref, aff_ref):
    d = x_ref.shape[1]
    x = x_ref[...]
    h = (_rms(x) * gn1_ref[...] * (1.0 + sc1_ref[...]) + sh1_ref[...]).astype(BF16)
    ga = 1.0 / (1.0 + jnp.exp(-(_dot(h, wgl_ref[:, :d]) + bg_ref[:, :d])))
    gb = 1.0 / (1.0 + jnp.exp(-(_dot(h, wgl_ref[:, d:]) + bg_ref[:, d:])))
    parts = []
    for hd in range(d // HEAD_V):
        cs = slice(hd * HEAD_V, (hd + 1) * HEAD_V)
        parts.append((ga[:, cs] * oa_ref[hd].astype(F32) + gb[:, cs] * ob_ref[hd].astype(F32)).astype(BF16))
    merged = jnp.concatenate(parts, axis=1)
    x1 = x + gt1_ref[...] * _dot(merged, wo_ref[...])
    x1_ref[...] = x1
    h2 = _rms(x1) * gn2_ref[...] * (1.0 + sc2_ref[...]) + sh2_ref[...]
    h_hi, h_lo = _split(h2)
    n_e = aff_ref.shape[0]
    both = _dot_nt(wr_ref[...], h_hi)
    lg = both[:n_e] + both[n_e:] + _dot_nt(wr_ref[:n_e, :], h_lo)
    e = jnp.exp(lg - jnp.max(lg, axis=0, keepdims=True))
    aff_ref[...] = e / jnp.sum(e, axis=0, keepdims=True)


def _out_proj(x, t_len, oa, ob, modr, mod_row, lw):
    n, d = x.shape
    tm = min(ROW_TM, t_len)
    tpb = t_len // tm
    nh = d // HEAD_V
    row = lambda: pl.BlockSpec((tm, d), lambda i: (i, 0))
    heads = lambda: pl.BlockSpec((None, nh, tm, HEAD_V), lambda i: (i // tpb, 0, i % tpb, 0))
    mod = lambda j: pl.BlockSpec((None, 1, d), lambda i: (mod_row(i * tm) * 6 + j, 0, 0))
    return pl.pallas_call(
        _out_proj_kernel,
        grid=(n // tm,),
        in_specs=[row(), heads(), heads(), mod(0), mod(1), mod(2), mod(3), mod(4),
                  _const_spec((1, d)), _const_spec((1, d)), _const_spec((d, 2 * d)),
                  _const_spec((1, 2 * d)), _const_spec((d, d)),
                  _const_spec((2 * N_EXPERTS, d))],
        out_specs=[row(), pl.BlockSpec((N_EXPERTS, tm), lambda i: (0, i))],
        out_shape=[jax.ShapeDtypeStruct((n, d), F32), jax.ShapeDtypeStruct((N_EXPERTS, n), F32)],
        compiler_params=_params(("arbitrary",)),
        name="out_proj",
    )(x, oa, ob, modr, modr, modr, modr, modr, lw["g_attn"], lw["g_ffn"], lw["w_gl"], lw["b_gate"],
      lw["w_out"], lw["wr"])


def _lane_cumsum(x_bf16, tri):
    t = x_bf16.shape[1]
    blk = tri.shape[0]
    parts = []
    carry = jnp.zeros((x_bf16.shape[0], 1), F32)
    for c in range(t // blk):
        part = _dot(x_bf16[:, c * blk:(c + 1) * blk], tri) + carry
        carry = part[:, blk - 1:blk]
        parts.append(part)
    return parts[0] if len(parts) == 1 else jnp.concatenate(parts, axis=1)


def _router_kernel(cap, slot_base, aff_ref, idx_ref, w_ref, pos_ref):
    r = pl.program_id(0)
    a = aff_ref[...]
    n_e, t = a.shape
    bits = pltpu.bitcast(a, jnp.int32)

    thr = jnp.zeros((n_e, 1), jnp.int32)
    for b in range(30, -1, -1):
        cand = thr | (1 << b)
        cnt = jnp.sum((bits >= cand).astype(jnp.int32), axis=1, keepdims=True)
        thr = jnp.where(cnt >= cap, cand, thr)
    gt = bits > thr
    eq = bits == thr
    need = (cap - jnp.sum(gt.astype(jnp.int32), axis=1, keepdims=True)).astype(F32)

    blk = min(MXU_DIM, t)
    ri = lax.broadcasted_iota(jnp.int32, (blk, blk), 0)
    ci = lax.broadcasted_iota(jnp.int32, (blk, blk), 1)
    tri = (ri <= ci).astype(BF16)
    cum_eq = _lane_cumsum(eq.astype(BF16), tri)
    sel = gt | (eq & (cum_eq <= need))
    cnt_incl = _lane_cumsum(sel.astype(BF16), tri)
    pad = jnp.zeros((LANES - n_e, t), F32)
    w_ref[...] = jnp.concatenate([jnp.where(sel, a, 0.0), pad], axis=0).T
    pos = cnt_incl - sel.astype(F32)
    pos_ref[...] = jnp.concatenate([pos, pad], axis=0).T.astype(jnp.int32) + (slot_base + r * cap)

    ts = min(LANES, cap)
    lane = lax.broadcasted_iota(jnp.int32, (1, LANES), 1)
    for sb in range(cap // ts):
        s_iota = (lax.broadcasted_iota(jnp.int32, (ts, 1), 0) + sb * ts).astype(F32)
        out = jnp.zeros((ts, LANES), F32)
        for e in range(n_e):
            below = jnp.where(cnt_incl[e:e + 1, :] <= s_iota, 1.0, 0.0)
            out = jnp.where(lane == e, jnp.sum(below, axis=1, keepdims=True), out)
        idx_ref[sb * ts:(sb + 1) * ts, :] = out.astype(jnp.int32) + r * t


def _router(aff_t, n_req, slot_base):
    n_e, n = aff_t.shape
    t = n // n_req
    cap = EC_FACTOR * t // N_EXPERTS
    return pl.pallas_call(
        functools.partial(_router_kernel, cap, slot_base),
        grid=(n_req,),
        in_specs=[pl.BlockSpec((n_e, t), lambda r: (0, r))],
        out_specs=[pl.BlockSpec((cap, LANES), lambda r: (r, 0)),
                   pl.BlockSpec((t, LANES), lambda r: (r, 0)),
                   pl.BlockSpec((t, LANES), lambda r: (r, 0))],
        out_shape=[jax.ShapeDtypeStruct((n_req * cap, LANES), jnp.int32),
                   jax.ShapeDtypeStruct((n, LANES), F32),
                   jax.ShapeDtypeStruct((n, LANES), jnp.int32)],
        compiler_params=_params(("arbitrary",)),
        name="router",
    )(aff_t)


def _ffn_kernel(ts, idx_ref, xp_hbm, xs_hbm, sh_ref, sc_ref, gn_ref, wg_ref, wu_ref, wd_ref, ye_ref,
                xbuf, sem):
    n_tiles = pl.num_programs(1)
    step = pl.program_id(0) * n_tiles + pl.program_id(1)
    last = pl.num_programs(0) * n_tiles - 1
    slot = step % 2
    nslot = 1 - slot
    d = xbuf.shape[2]

    def row_copy(src, tile, s, buf):
        return pltpu.make_async_copy(src.at[pl.ds(idx_ref[tile * ts + s], 1), :],
                                     xbuf.at[buf, pl.ds(s, 1), :], sem.at[buf])

    def wait_tile(buf):
        pltpu.make_async_copy(xs_hbm.at[pl.ds(0, ts), :], xbuf.at[buf], sem.at[buf]).wait()

    @pl.when(step == 0)
    def _():
        def body(s, carry):
            row_copy(xp_hbm, 0, s, 0).start()
            return carry
        lax.fori_loop(0, ts, body, 0, unroll=8)

    wait_tile(slot)
    nxt = jnp.minimum(step + 1, last)
    n_groups = FFN_GATE_CHUNKS + FFN_DOWN_CHUNKS
    per_group = ts // n_groups

    def run(src):
        issued = [0]

        def issue_group():
            for s in range(issued[0], issued[0] + per_group):
                row_copy(src, nxt, s, nslot).start()
            issued[0] += per_group

        h = (_rms(xbuf[slot]) * gn_ref[...] * (1.0 + sc_ref[...]) + sh_ref[...]).astype(BF16)
        fw = D_FF // FFN_GATE_CHUNKS
        hid = []
        for c in range(FFN_GATE_CHUNKS):
            issue_group()
            gate = _dot(h, wg_ref[:, c * fw:(c + 1) * fw])
            up = _dot(h, wu_ref[:, c * fw:(c + 1) * fw])
            hid.append((gate / (1.0 + jnp.exp(-gate)) * up).astype(BF16))
        hid = jnp.concatenate(hid, axis=1)
        dw = d // FFN_DOWN_CHUNKS
        for c in range(FFN_DOWN_CHUNKS):
            issue_group()
            ye_ref[:, c * dw:(c + 1) * dw] = _dot(hid, wd_ref[:, c * dw:(c + 1) * dw]).astype(BF16)

    next_is_context = nxt % n_tiles == 0

    @pl.when(next_is_context)
    def _():
        run(xp_hbm)

    @pl.when(jnp.logical_not(next_is_context))
    def _():
        run(xs_hbm)

    @pl.when(step == last)
    def _():
        wait_tile(nslot)


def _ffn(idx, x1p, x1s, modr, layer, lw, n_tiles, ts):
    d = x1p.shape[1]
    assert ts % (FFN_GATE_CHUNKS + FFN_DOWN_CHUNKS) == 0
    assert D_FF % FFN_GATE_CHUNKS == 0 and d % FFN_DOWN_CHUNKS == 0
    mod = lambda c: pl.BlockSpec((None, 1, d), lambda e, j, *_: ((layer * 8 + j) * 6 + c, 0, 0))
    grid_spec = pltpu.PrefetchScalarGridSpec(
        num_scalar_prefetch=1,
        grid=(N_EXPERTS, n_tiles),
        in_specs=[
            pl.BlockSpec(memory_space=pl.ANY), pl.BlockSpec(memory_space=pl.ANY), mod(3), mod(4),
            pl.BlockSpec((1, d), lambda e, j, *_: (0, 0)),
            pl.BlockSpec((None, d, D_FF), lambda e, j, *_: (e, 0, 0)),
            pl.BlockSpec((None, d, D_FF), lambda e, j, *_: (e, 0, 0)),
            pl.BlockSpec((None, D_FF, d), lambda e, j, *_: (e, 0, 0)),
        ],
        out_specs=pl.BlockSpec((None, ts, d), lambda e, j, *_: (e, j, 0)),
        scratch_shapes=[pltpu.VMEM((2, ts, d), F32), pltpu.SemaphoreType.DMA((2,))],
    )
    return pl.pallas_call(
        functools.partial(_ffn_kernel, ts),
        grid_spec=grid_spec,
        out_shape=jax.ShapeDtypeStruct((N_EXPERTS, n_tiles * ts, d), BF16),
        compiler_params=_params(("arbitrary", "arbitrary")),
        name="ffn",
    )(idx, x1p, x1s, modr, modr, lw["g_ffn"], lw["w_gate"], lw["w_up"], lw["w_down"])


def _combine_kernel(final, tm, slots, starts_ref, ye_hbm, x_ref, pos_ref, w_ref, gt_ref, gf_ref, o_ref,
                    stage, stage_x, acc_ref, sem, sem_x):
    i = pl.program_id(0)
    last = pl.num_programs(0) - 1
    n_e, wn = N_EXPERTS, COMBINE_W
    per_pass = MXU_DIM // wn
    top = n_e * slots - wn
    slot = i % 2
    nslot = 1 - slot

    def window(tile, e, c):
        s_lo = starts_ref[tile * n_e + e]
        nominal = e * slots + (s_lo // BF16_ROWS) * BF16_ROWS + c * wn
        return jnp.minimum(nominal, top), nominal

    def win_copy(tile, e, c, dst, s):
        base, _ = window(tile, e, c)
        return pltpu.make_async_copy(ye_hbm.at[pl.ds(pl.multiple_of(base, BF16_ROWS), wn), :],
                                     dst.at[pl.ds(e * wn, wn), :], s)

    @pl.when(i == 0)
    def _():
        for e in range(n_e):
            win_copy(0, e, 0, stage.at[0], sem.at[0]).start()

    pltpu.make_async_copy(ye_hbm.at[pl.ds(0, n_e * wn), :], stage.at[slot], sem.at[slot]).wait()
    nxt = jnp.minimum(i + 1, last)
    for e in range(n_e):
        win_copy(nxt, e, 0, stage.at[nslot], sem.at[nslot]).start()

    pos = pos_ref[...]
    w = w_ref[...]
    lane = lax.broadcasted_iota(jnp.int32, (1, MXU_DIM), 1)

    def accumulate(src, c, first):
        for g in range(n_e // per_pass):
            tgt = val = None
            for j in range(per_pass):
                e = g * per_pass + j
                base, nominal = window(i, e, c)
                gs = pos[:, e:e + 1] + e * slots
                t_j = gs - (base - j * wn)
                in_round = (gs >= nominal) & (gs < nominal + wn)
                v_j = jnp.where(in_round, w[:, e:e + 1], 0.0)
                if j == 0:
                    tgt, val = t_j, v_j
                else:
                    here = lane >= j * wn
                    tgt = jnp.where(here, t_j, tgt)
                    val = jnp.where(here, v_j, val)
            place = jnp.where(tgt == lane, val, 0.0).astype(BF16)
            part = _dot(place, src[g * MXU_DIM:(g + 1) * MXU_DIM, :])
            if first and g == 0:
                acc_ref[...] = part
            else:
                acc_ref[...] += part

    accumulate(stage.at[slot], 0, True)

    rounds = 1
    for e in range(n_e):
        s_lo = starts_ref[i * n_e + e]
        span = starts_ref[(i + 1) * n_e + e] - (s_lo // BF16_ROWS) * BF16_ROWS
        rounds = jnp.maximum(rounds, (span + wn - 1) // wn)

    def extra_round(c, carry):
        for e in range(n_e):
            win_copy(i, e, c, stage_x, sem_x).start()
        pltpu.make_async_copy(ye_hbm.at[pl.ds(0, n_e * wn), :], stage_x, sem_x).wait()
        accumulate(stage_x, c, False)
        return carry

    lax.fori_loop(1, rounds, extra_round, 0)

    y = x_ref[...] + gt_ref[...] * acc_ref[...]
    if final:
        y = _rms(y) * gf_ref[...]
    o_ref[...] = y

    @pl.when(i == last)
    def _():
        pltpu.make_async_copy(ye_hbm.at[pl.ds(0, n_e * wn), :], stage.at[nslot], sem.at[nslot]).wait()


def _combine(starts, ye, x1, pos, w, modr, mod_row, g_final, *, slots, final):
    n, d = x1.shape
    tm = COMBINE_TM
    assert MXU_DIM % COMBINE_W == 0 and N_EXPERTS % (MXU_DIM // COMBINE_W) == 0
    grid_spec = pltpu.PrefetchScalarGridSpec(
        num_scalar_prefetch=1,
        grid=(n // tm,),
        in_specs=[
            pl.BlockSpec(memory_space=pl.ANY),
            pl.BlockSpec((tm, d), lambda i, *_: (i, 0)),
            pl.BlockSpec((tm, LANES), lambda i, *_: (i, 0)),
            pl.BlockSpec((tm, LANES), lambda i, *_: (i, 0)),
            pl.BlockSpec((None, 1, d), lambda i, *_: (mod_row(i * tm) * 6 + 5, 0, 0)),
            pl.BlockSpec((1, d), lambda i, *_: (0, 0)),
        ],
        out_specs=pl.BlockSpec((tm, d), lambda i, *_: (i, 0)),
        scratch_shapes=[pltpu.VMEM((2, N_EXPERTS * COMBINE_W, d), BF16),
                        pltpu.VMEM((N_EXPERTS * COMBINE_W, d), BF16),
                        pltpu.VMEM((tm, d), F32),
                        pltpu.SemaphoreType.DMA((2,)), pltpu.SemaphoreType.DMA(())],
    )
    return pl.pallas_call(
        functools.partial(_combine_kernel, final, tm, slots),
        grid_spec=grid_spec,
        out_shape=jax.ShapeDtypeStruct((n, d), F32),
        compiler_params=_params(("arbitrary",)),
        name="combine_final" if final else "combine",
    )(starts, ye, x1, pos, w, modr, g_final)


def _rope_tables(n_tokens, rot_dim):
    t = jnp.arange(n_tokens)
    row = (t // GRID_W).astype(F32)
    col = (t % GRID_W).astype(F32)
    quarter = rot_dim // 4
    inv = ROPE_THETA ** (-jnp.arange(quarter, dtype=F32) / quarter)
    ar, ac = row[:, None] * inv, col[:, None] * inv
    cos = jnp.concatenate([jnp.cos(ar), jnp.cos(ar), jnp.cos(ac), jnp.cos(ac)], axis=-1)
    sin = jnp.concatenate([-jnp.sin(ar), jnp.sin(ar), -jnp.sin(ac), jnp.sin(ac)], axis=-1)
    pad = LANES - rot_dim
    if pad:
        cos = jnp.concatenate([cos, jnp.ones((n_tokens, pad), F32)], axis=-1)
        sin = jnp.concatenate([sin, jnp.zeros((n_tokens, pad), F32)], axis=-1)
    return cos, sin


def _repack_kernel(o_kr, o_gl, w_ref, wa_ref, wgl_ref):
    rows = w_ref.shape[0]
    wa_ref[:, :C_KR] = w_ref[:, :C_KR].astype(BF16)
    wa_ref[:, C_KR:C_GQ] = jnp.concatenate(
        [w_ref[:, C_KR:o_kr], jnp.zeros((rows, C_GQ - C_KR - MLA_ROPE), F32)], axis=1).astype(BF16)
    wa_ref[:, C_GQ:] = w_ref[:, o_kr:o_gl].astype(BF16)
    wgl_ref[...] = w_ref[:, o_gl:].astype(BF16)


def _repack_w_in(w_in, l):
    _, d, cols = w_in.shape
    o_kr = MLA_Q_RANK + MLA_KV_RANK + MLA_ROPE
    o_gl = o_kr + (GQA_HEADS + 2 * GQA_KV_HEADS) * GQA_HEAD_DIM
    tm = 256
    return pl.pallas_call(
        functools.partial(_repack_kernel, o_kr, o_gl),
        grid=(d // tm,),
        in_specs=[pl.BlockSpec((None, tm, cols), lambda i: (l, i, 0))],
        out_specs=[pl.BlockSpec((tm, C_END), lambda i: (i, 0)),
                   pl.BlockSpec((tm, cols - o_gl), lambda i: (i, 0))],
        out_shape=[jax.ShapeDtypeStruct((d, C_END), BF16), jax.ShapeDtypeStruct((d, cols - o_gl), BF16)],
        compiler_params=_params(("arbitrary",)),
        name="repack_w_in",
    )(w_in)


def _layer_weights(l, g_attn_norm, w_in, b_gate, g_mla_q, w_mla_uq, g_mla_kv, w_mla_uk, w_mla_uv,
                   g_gqa_q, g_gqa_k, w_out, g_ffn_norm, w_router, w_gate_e, w_up_e, w_down_e):
    w_a, w_gl = _repack_w_in(w_in, l)
    w_uq = w_mla_uq[l].reshape(MLA_Q_RANK, MLA_HEADS, MLA_QK)
    w_uq = jnp.pad(w_uq, ((0, 0), (0, 0), (0, MLA_QK_PAD - MLA_QK))).reshape(MLA_Q_RANK, -1).astype(BF16)
    wr = w_router[l].T
    wr_hi = wr.astype(BF16)
    wr_lo = (wr - wr_hi.astype(F32)).astype(BF16)
    return {
        "g_attn": g_attn_norm[l][None], "g_ffn": g_ffn_norm[l][None], "w_a": w_a,
        "g_mla_q": g_mla_q[l][None], "w_uq": w_uq, "g_mla_kv": g_mla_kv[l][None],
        "g_gqa_q": g_gqa_q[l][None], "g_gqa_k": g_gqa_k[l][None],
        "w_uk": w_mla_uk[l].astype(BF16), "w_uv": w_mla_uv[l].astype(BF16),
        "w_gl": w_gl, "b_gate": b_gate[l][None], "w_out": w_out[l].astype(BF16),
        "wr": jnp.concatenate([wr_hi, wr_lo], axis=0),
        "w_gate": w_gate_e[l].astype(BF16), "w_up": w_up_e[l].astype(BF16),
        "w_down": w_down_e[l].astype(BF16),
    }


def _combine_starts(pos, slot_end):
    starts = jnp.concatenate([pos[::COMBINE_TM, :N_EXPERTS],
                              jnp.full((1, N_EXPERTS), slot_end, jnp.int32)], axis=0)
    return starts.reshape(-1)


def kernel(x_prompt, x_sample, c, cache_mla_ckv, cache_mla_krope, cache_gqa_k, cache_gqa_v, c_ctx,
           w_ada, b_ada, g_attn_norm, w_in, b_gate, g_mla_q, w_mla_uq, g_mla_kv, w_mla_uk, w_mla_uv,
           g_gqa_q, g_gqa_k, w_out, g_ffn_norm, w_router, w_gate_e, w_up_e, w_down_e, g_final):
    bp, tp, d = x_prompt.shape
    bs, tl, _ = x_sample.shape
    n_layers = w_ada.shape[0]
    np_, ns = bp * tp, bs * tl
    kvw = GQA_KV_HEADS * GQA_HEAD_DIM
    cap_p = EC_FACTOR * tp // N_EXPERTS
    cap_s = EC_FACTOR * tl // N_EXPERTS
    ts = cap_s
    assert bp * cap_p == ts and bs + 1 <= 8
    n_tiles = 1 + bs
    slots = n_tiles * ts

    cc = jnp.zeros((8, d), F32).at[0].set(c_ctx).at[1:1 + bs].set(c)
    modr = _ada(cc, w_ada, b_ada).reshape(n_layers * 8 * 6, 1, d)

    cos_m, sin_m = _rope_tables(tl, MLA_ROPE)
    cos_g, sin_g = _rope_tables(tl, GQA_HEAD_DIM)
    tabs = (cos_m, sin_m, cos_g, sin_g)
    g_fin = g_final[None]

    xp = x_prompt.reshape(np_, d)
    xs = x_sample.reshape(ns, d)
    st = {"ckv": [], "kr": [], "k": [], "v": []}
    for l in range(n_layers):
        lw = _layer_weights(l, g_attn_norm, w_in, b_gate, g_mla_q, w_mla_uq, g_mla_kv, w_mla_uk,
                            w_mla_uv, g_gqa_q, g_gqa_k, w_out, g_ffn_norm, w_router, w_gate_e,
                            w_up_e, w_down_e)
        row_p = lambda r, l=l: l * 8
        row_s = lambda r, l=l: l * 8 + 1 + r // tl

        qm, ckv, kr, qg, kg, vg, ckv32, kr32, kg32, vg32 = _qkv(
            xp, tp, modr, row_p, lw, tabs, rope=False, cache_out=True)
        st["ckv"].append(ckv32.reshape(bp, tp, MLA_KV_RANK))
        st["kr"].append(kr32.reshape(bp, tp, MLA_ROPE))
        st["k"].append(kg32.reshape(bp, tp, GQA_KV_HEADS, GQA_HEAD_DIM))
        st["v"].append(vg32.reshape(bp, tp, GQA_KV_HEADS, GQA_HEAD_DIM))
        k_m, v_m = _kv_expand(ckv.reshape(bp, tp, -1), kr.reshape(bp, tp, -1), lw["w_uk"], lw["w_uv"])
        oa_p = _attn(qm, k_m, v_m, hps=MLA_HEADS, rolled=False)
        ob_p = _attn(qg.reshape(bp, GQA_KV_HEADS, GQA_GROUP * tp, GQA_HEAD_DIM), kg.reshape(bp, tp, -1),
                     vg.reshape(bp, tp, -1), hps=GQA_KV_HEADS, rolled=False)
        x1p, aff_p = _out_proj(xp, tp, oa_p, ob_p.reshape(bp, GQA_HEADS, tp, HEAD_V), modr, row_p, lw)

        qm, ckv, kr, qg, kg, vg = _qkv(xs, tl, modr, row_s, lw, tabs, rope=True, cache_out=False)
        kr_cache = jnp.pad(cache_mla_krope[:, l], ((0, 0), (0, 0), (0, LANES - MLA_ROPE))).astype(BF16)
        ckv_all = jnp.concatenate([cache_mla_ckv[:, l].astype(BF16), ckv.reshape(bs, tl, -1)], axis=1)
        kr_all = jnp.concatenate([kr_cache, kr.reshape(bs, tl, -1)], axis=1)
        k_m, v_m = _kv_expand(ckv_all, kr_all, lw["w_uk"], lw["w_uv"])
        oa_s = _attn(qm, k_m, v_m, hps=1, rolled=True)
        past = cache_gqa_k.shape[2]
        k_all = jnp.concatenate([cache_gqa_k[:, l].reshape(bs, past, kvw).astype(BF16),
                                 kg.reshape(bs, tl, kvw)], axis=1)
        v_all = jnp.concatenate([cache_gqa_v[:, l].reshape(bs, past, kvw).astype(BF16),
                                 vg.reshape(bs, tl, kvw)], axis=1)
        ob_s = _attn(qg.reshape(bs, GQA_KV_HEADS, GQA_GROUP * tl, GQA_HEAD_DIM), k_all, v_all,
                     hps=1, rolled=True)
        x1s, aff_s = _out_proj(xs, tl, oa_s, ob_s.reshape(bs, GQA_HEADS, tl, HEAD_V), modr, row_s, lw)

        idx_p, w_p, pos_p = _router(aff_p, bp, 0)
        idx_s, w_s, pos_s = _router(aff_s, bs, ts)
        idx = jnp.concatenate([idx_p[:, :N_EXPERTS], idx_s[:, :N_EXPERTS]], axis=0).T.reshape(-1)
        ye = _ffn(idx, x1p, x1s, modr, l, lw, n_tiles, ts).reshape(N_EXPERTS * slots, d)
        final = l == n_layers - 1
        xp = _combine(_combine_starts(pos_p, ts), ye, x1p, pos_p, w_p, modr, row_p, g_fin,
                      slots=slots, final=final)
        xs = _combine(_combine_starts(pos_s, slots), ye, x1s, pos_s, w_s, modr, row_s, g_fin,
                      slots=slots, final=final)

    return (xp.reshape(bp, tp, d), xs.reshape(bs, tl, d),
            jnp.stack(st["ckv"], axis=1), jnp.stack(st["kr"], axis=1),
            jnp.stack(st["k"], axis=1), jnp.stack(st["v"], axis=1))
```

```python
import functools

import jax
import jax.numpy as jnp
from jax import lax
from jax.experimental import pallas as pl
from jax.experimental.pallas import tpu as pltpu

F32 = jnp.float32
BF16 = jnp.bfloat16

D_MODEL = 2048
GRID_W = 64
ROPE_THETA = 10000.0
EPS = 1e-6
MLA_HEADS = 16
MLA_Q_RANK = 512
MLA_KV_RANK = 256
MLA_NOPE = 128
MLA_ROPE = 64
MLA_V = 128
MLA_QK = MLA_NOPE + MLA_ROPE
MLA_SCALE = MLA_QK ** -0.5
GQA_HEADS = 16
GQA_KV_HEADS = 4
GQA_GROUP = GQA_HEADS // GQA_KV_HEADS
GQA_HEAD_DIM = 128
GQA_SCALE = GQA_HEAD_DIM ** -0.5
HEAD_V = 128
N_EXPERTS = 16
EC_FACTOR = 2
D_FF = 1024
LOG2E = 1.4426950408889634

LANES = 128
BF16_ROWS = 16
MXU_DIM = 256
MLA_QK_PAD = MXU_DIM
VMEM_LIMIT = 56 * 1024 * 1024

ROW_TM = 256
ATTN_TQ = 512
ATTN_TK = 512
COMBINE_TM = 256
COMBINE_W = 64
FFN_GATE_CHUNKS = 4
FFN_DOWN_CHUNKS = 4

C_CQ = 0
C_CKV = C_CQ + MLA_Q_RANK
C_KR = C_CKV + MLA_KV_RANK
C_GQ = C_KR + LANES
C_GK = C_GQ + GQA_HEADS * GQA_HEAD_DIM
C_GV = C_GK + GQA_KV_HEADS * GQA_HEAD_DIM
C_END = C_GV + GQA_KV_HEADS * GQA_HEAD_DIM


def _dot(a, b):
    return jnp.dot(a, b, preferred_element_type=F32)


def _dot_nt(a, b):
    return lax.dot_general(a, b, (((1,), (1,)), ((), ())), preferred_element_type=F32)


def _split(x):
    hi = x.astype(BF16)
    lo = (x - hi.astype(F32)).astype(BF16)
    return hi, lo


def _rms(x):
    return x * lax.rsqrt(jnp.mean(x * x, axis=-1, keepdims=True) + EPS)


def _params(sem):
    return pltpu.CompilerParams(dimension_semantics=sem, vmem_limit_bytes=VMEM_LIMIT)


def _const_spec(shape):
    nd = len(shape)
    return pl.BlockSpec(shape, lambda *_: (0,) * nd, pipeline_mode=pl.Buffered(1))


def _ada_kernel(c_ref, w_ref, b_ref, o_ref):
    c = c_ref[...]
    s = c / (1.0 + jnp.exp(-c))
    rows = c.shape[0]
    s_hi, s_lo = _split(s)
    w_hi, w_lo = _split(w_ref[...])
    both = _dot(jnp.concatenate([s_hi, s_lo], axis=0), w_hi)
    o_ref[...] = both[:rows] + both[rows:] + _dot(s_hi, w_lo) + b_ref[...]


def _ada(cc, w_ada, b_ada):
    n_layers, d, n6 = w_ada.shape
    rows = cc.shape[0]
    tn = 1024
    return pl.pallas_call(
        _ada_kernel,
        grid=(n_layers, n6 // tn),
        in_specs=[
            pl.BlockSpec((rows, d), lambda l, j: (0, 0)),
            pl.BlockSpec((None, d, tn), lambda l, j: (l, 0, j)),
            pl.BlockSpec((None, 1, tn), lambda l, j: (l, 0, j)),
        ],
        out_specs=pl.BlockSpec((None, rows, tn), lambda l, j: (l, 0, j)),
        out_shape=jax.ShapeDtypeStruct((n_layers, rows, n6), F32),
        compiler_params=_params(("arbitrary", "arbitrary")),
        name="ada",
    )(cc, w_ada, b_ada.reshape(n_layers, 1, n6))


def _swap_halves(x, q):
    lane = lax.broadcasted_iota(jnp.int32, x.shape, 1)
    first = (lane % (2 * q)) < q
    return jnp.where(first, pltpu.roll(x, LANES - q, 1), pltpu.roll(x, q, 1))


def _qkv_kernel(rope, cache_out, x_ref, sh_ref, sc_ref, gn_ref, w_ref, gq_ref, wuq_ref, gkv_ref,
                ggq_ref, ggk_ref, cm_ref, sm_ref, cg_ref, sg_ref, *outs):
    qm_ref, ckv_ref, kr_ref, qg_ref, kg_ref, vg_ref = outs[:6]
    x = x_ref[...]
    h = (_rms(x) * gn_ref[...] * (1.0 + sc_ref[...]) + sh_ref[...]).astype(BF16)

    cq = _rms(_dot(h, w_ref[:, C_CQ:C_CKV])) * gq_ref[...]
    qm = _dot(cq.astype(BF16), wuq_ref[...]) * (MLA_SCALE * LOG2E)
    for hd in range(MLA_HEADS):
        lo = hd * MLA_QK_PAD
        qm_ref[hd, :, :LANES] = qm[:, lo:lo + LANES].astype(BF16)
        t = qm[:, lo + LANES:lo + 2 * LANES]
        if rope:
            t = t * cm_ref[...] + _swap_halves(t, MLA_ROPE // 4) * sm_ref[...]
        qm_ref[hd, :, LANES:] = t.astype(BF16)

    ckv = _rms(_dot(h, w_ref[:, C_CKV:C_KR])) * gkv_ref[...]
    ckv_ref[...] = ckv.astype(BF16)
    kr = _dot(h, w_ref[:, C_KR:C_GQ])
    if cache_out:
        outs[6][...] = ckv
        outs[7][...] = kr[:, :MLA_ROPE]
    if rope:
        kr = kr * cm_ref[...] + _swap_halves(kr, MLA_ROPE // 4) * sm_ref[...]
    kr_ref[...] = kr.astype(BF16)

    zq = _dot(h, w_ref[:, C_GQ:C_GK])
    for hd in range(GQA_HEADS):
        lo = hd * GQA_HEAD_DIM
        t = _rms(zq[:, lo:lo + GQA_HEAD_DIM]) * ggq_ref[...]
        if rope:
            t = t * cg_ref[...] + _swap_halves(t, GQA_HEAD_DIM // 4) * sg_ref[...]
        qg_ref[hd] = (t * (GQA_SCALE * LOG2E)).astype(BF16)
    zk = _dot(h, w_ref[:, C_GK:C_GV])
    for hd in range(GQA_KV_HEADS):
        lo = hd * GQA_HEAD_DIM
        t = _rms(zk[:, lo:lo + GQA_HEAD_DIM]) * ggk_ref[...]
        if cache_out:
            outs[8][:, lo:lo + GQA_HEAD_DIM] = t
        if rope:
            t = t * cg_ref[...] + _swap_halves(t, GQA_HEAD_DIM // 4) * sg_ref[...]
        kg_ref[:, lo:lo + GQA_HEAD_DIM] = t.astype(BF16)
    zv = _dot(h, w_ref[:, C_GV:C_END])
    vg_ref[...] = zv.astype(BF16)
    if cache_out:
        outs[9][...] = zv


def _qkv(x, t_len, modr, mod_row, lw, tabs, *, rope, cache_out):
    n, d = x.shape
    b = n // t_len
    tm = min(ROW_TM, t_len)
    tpb = t_len // tm
    pos_tiles = tabs[0].shape[0] // tm
    kvw = GQA_KV_HEADS * GQA_HEAD_DIM
    row = lambda w: pl.BlockSpec((tm, w), lambda i: (i, 0))
    row_in = row(d)
    heads = lambda nh, w: pl.BlockSpec((None, nh, tm, w), lambda i: (i // tpb, 0, i % tpb, 0))
    mod = lambda j: pl.BlockSpec((None, 1, d), lambda i: (mod_row(i * tm) * 6 + j, 0, 0))
    tab = lambda w: pl.BlockSpec((tm, w), lambda i: (i % pos_tiles, 0))
    in_specs = [
        row_in, mod(0), mod(1), _const_spec((1, d)), _const_spec((d, C_END)),
        _const_spec((1, MLA_Q_RANK)), _const_spec((MLA_Q_RANK, MLA_HEADS * MLA_QK_PAD)),
        _const_spec((1, MLA_KV_RANK)), _const_spec((1, GQA_HEAD_DIM)), _const_spec((1, GQA_HEAD_DIM)),
        tab(LANES), tab(LANES), tab(LANES), tab(LANES),
    ]
    out_shape = [
        jax.ShapeDtypeStruct((b, MLA_HEADS, t_len, MLA_QK_PAD), BF16),
        jax.ShapeDtypeStruct((n, MLA_KV_RANK), BF16),
        jax.ShapeDtypeStruct((n, LANES), BF16),
        jax.ShapeDtypeStruct((b, GQA_HEADS, t_len, GQA_HEAD_DIM), BF16),
        jax.ShapeDtypeStruct((n, kvw), BF16),
        jax.ShapeDtypeStruct((n, kvw), BF16),
    ]
    out_specs = [heads(MLA_HEADS, MLA_QK_PAD), row(MLA_KV_RANK), row(LANES),
                 heads(GQA_HEADS, GQA_HEAD_DIM), row(kvw), row(kvw)]
    if cache_out:
        out_shape += [
            jax.ShapeDtypeStruct((n, MLA_KV_RANK), F32),
            jax.ShapeDtypeStruct((n, MLA_ROPE), F32),
            jax.ShapeDtypeStruct((n, kvw), F32),
            jax.ShapeDtypeStruct((n, kvw), F32),
        ]
        out_specs += [row(MLA_KV_RANK), row(MLA_ROPE), row(kvw), row(kvw)]
    return pl.pallas_call(
        functools.partial(_qkv_kernel, rope, cache_out),
        grid=(n // tm,),
        in_specs=in_specs,
        out_specs=out_specs,
        out_shape=out_shape,
        compiler_params=_params(("arbitrary",)),
        name="qkv_rope" if rope else "qkv_ctx",
    )(x, modr, modr, lw["g_attn"], lw["w_a"], lw["g_mla_q"], lw["w_uq"], lw["g_mla_kv"],
      lw["g_gqa_q"], lw["g_gqa_k"], *tabs)


def _kv_expand_kernel(ckv_ref, kr_ref, wuk_ref, wuv_ref, k_ref, v_ref):
    ckv = ckv_ref[...]
    kn = _dot(ckv, wuk_ref[...]).astype(BF16)
    kr = kr_ref[...]
    for hd in range(MLA_HEADS):
        k_ref[:, hd * MLA_QK_PAD:hd * MLA_QK_PAD + LANES] = kn[:, hd * MLA_NOPE:(hd + 1) * MLA_NOPE]
        k_ref[:, hd * MLA_QK_PAD + LANES:(hd + 1) * MLA_QK_PAD] = kr
    v_ref[...] = _dot(ckv, wuv_ref[...]).astype(BF16)


def _kv_expand(ckv, kr, w_uk, w_uv):
    b, s, _ = ckv.shape
    ts = min(512, s)
    kw, vw = MLA_HEADS * MLA_QK_PAD, MLA_HEADS * MLA_V
    blk = lambda w: pl.BlockSpec((None, ts, w), lambda i, j: (i, j, 0))
    return pl.pallas_call(
        _kv_expand_kernel,
        grid=(b, s // ts),
        in_specs=[blk(MLA_KV_RANK), blk(LANES), _const_spec((MLA_KV_RANK, MLA_HEADS * MLA_NOPE)),
                  _const_spec((MLA_KV_RANK, vw))],
        out_specs=[blk(kw), blk(vw)],
        out_shape=[jax.ShapeDtypeStruct((b, s, kw), BF16), jax.ShapeDtypeStruct((b, s, vw), BF16)],
        compiler_params=_params(("arbitrary", "arbitrary")),
        name="kv_expand",
    )(ckv, kr, w_uk, w_uv)


def _attn_kernel(tq, tk, dq, hps, rolled, q_ref, k_ref, v_ref, o_ref, s0, s1, m0, m1, vx_ref):
    s_len = k_ref.shape[0]
    rows = q_ref.shape[1]
    n_chunks = s_len // tk
    dv = HEAD_V
    bufs = ((s0, m0), (s1, m1))

    for hh in range(hps):
        vx_ref[hh, :, :dv] = v_ref[:, hh * dv:(hh + 1) * dv]
        vx_ref[hh, :, dv:] = jnp.ones((s_len, dv), BF16)

    def stage(p1, p2):
        if p1 is not None:
            h1, r1, b1 = p1
            q = q_ref[h1, pl.ds(r1, tq), :]
            sb1, mb1 = bufs[b1]
        if p2 is not None:
            h2, r2, b2 = p2
            sb2, mb2 = bufs[b2]
            m = jnp.max(mb2[...], axis=1, keepdims=True)
            acc = None
        for c in range(n_chunks):
            ck = slice(c * tk, (c + 1) * tk)
            if p1 is not None:
                s = _dot_nt(q, k_ref[ck, h1 * dq:(h1 + 1) * dq])
                sb1[:, ck] = s
                mx = s[:, :LANES]
                for t in range(1, tk // LANES):
                    mx = jnp.maximum(mx, s[:, t * LANES:(t + 1) * LANES])
                mb1[...] = mx if c == 0 else jnp.maximum(mb1[...], mx)
            if p2 is not None:
                p = jnp.exp2(sb2[:, ck] - m).astype(BF16)
                part = _dot(p, vx_ref[h2, ck, :])
                acc = part if acc is None else acc + part
        if p2 is not None:
            o_ref[h2, pl.ds(r2, tq), :] = (acc[:, :dv] / acc[:, dv:]).astype(BF16)

    n_units = rows // tq
    if rolled:
        stage((0, 0, 0), None)

        def body(i, carry):
            u = 2 * i
            stage((0, pl.multiple_of((u + 1) * tq, tq), 1), (0, pl.multiple_of(u * tq, tq), 0))
            nxt = jnp.minimum(u + 2, n_units - 1)
            stage((0, pl.multiple_of(nxt * tq, tq), 0), (0, pl.multiple_of((u + 1) * tq, tq), 1))
            return carry

        lax.fori_loop(0, n_units // 2, body, 0)
    else:
        units = [(hh, u * tq) for hh in range(hps) for u in range(n_units)]
        for j in range(len(units) + 1):
            p1 = units[j] + (j % 2,) if j < len(units) else None
            p2 = units[j - 1] + ((j - 1) % 2,) if j > 0 else None
            stage(p1, p2)


def _attn(q, k, v, *, hps, rolled):
    b, hk, rows, dq = q.shape
    s = k.shape[1]
    dv = HEAD_V
    tq = min(ATTN_TQ, rows)
    tk = min(ATTN_TK, s)
    assert rows % tq == 0 and s % tk == 0 and hk % hps == 0
    assert not rolled or (hps == 1 and (rows // tq) % 2 == 0)
    return pl.pallas_call(
        functools.partial(_attn_kernel, tq, tk, dq, hps, rolled),
        grid=(b, hk // hps),
        in_specs=[
            pl.BlockSpec((None, hps, rows, dq), lambda i, h: (i, h, 0, 0)),
            pl.BlockSpec((None, s, hps * dq), lambda i, h: (i, 0, h)),
            pl.BlockSpec((None, s, hps * dv), lambda i, h: (i, 0, h)),
        ],
        out_specs=pl.BlockSpec((None, hps, rows, dv), lambda i, h: (i, h, 0, 0)),
        out_shape=jax.ShapeDtypeStruct((b, hk, rows, dv), BF16),
        scratch_shapes=[pltpu.VMEM((tq, s), F32), pltpu.VMEM((tq, s), F32),
                        pltpu.VMEM((tq, LANES), F32), pltpu.VMEM((tq, LANES), F32),
                        pltpu.VMEM((hps, s, 2 * dv), BF16)],
        compiler_params=_params(("arbitrary", "arbitrary")),
        name="attn_rolled" if rolled else "attn_flat",
    )(q, k, v)


def _out_proj_kernel(x_ref, oa_ref, ob_ref, sh1_ref, sc1_ref, gt1_ref, sh2_ref, sc2_ref, gn1_ref,
                     gn2_ref, wgl_ref, bg_ref, wo_ref, wr_ref, x1_ref, aff_ref):
    d = x_ref.shape[1]
    x = x_ref[...]
    h = (_rms(x) * gn1_ref[...] * (1.0 + sc1_ref[...]) + sh1_ref[...]).astype(BF16)
    ga = 1.0 / (1.0 + jnp.exp(-(_dot(h, wgl_ref[:, :d]) + bg_ref[:, :d])))
    gb = 1.0 / (1.0 + jnp.exp(-(_dot(h, wgl_ref[:, d:]) + bg_ref[:, d:])))
    parts = []
    for hd in range(d // HEAD_V):
        cs = slice(hd * HEAD_V, (hd + 1) * HEAD_V)
        parts.append((ga[:, cs] * oa_ref[hd].astype(F32) + gb[:, cs] * ob_ref[hd].astype(F32)).astype(BF16))
    merged = jnp.concatenate(parts, axis=1)
    x1 = x + gt1_ref[...] * _dot(merged, wo_ref[...])
    x1_ref[...] = x1
    h2 = _rms(x1) * gn2_ref[...] * (1.0 + sc2_ref[...]) + sh2_ref[...]
    h_hi, h_lo = _split(h2)
    n_e = aff_ref.shape[0]
    both = _dot_nt(wr_ref[...], h_hi)
    lg = both[:n_e] + both[n_e:] + _dot_nt(wr_ref[:n_e, :], h_lo)
    e = jnp.exp(lg - jnp.max(lg, axis=0, keepdims=True))
    aff_ref[...] = e / jnp.sum(e, axis=0, keepdims=True)


def _out_proj(x, t_len, oa, ob, modr, mod_row, lw):
    n, d = x.shape
    tm = min(ROW_TM, t_len)
    tpb = t_len // tm
    nh = d // HEAD_V
    row = lambda: pl.BlockSpec((tm, d), lambda i: (i, 0))
    heads = lambda: pl.BlockSpec((None, nh, tm, HEAD_V), lambda i: (i // tpb, 0, i % tpb, 0))
    mod = lambda j: pl.BlockSpec((None, 1, d), lambda i: (mod_row(i * tm) * 6 + j, 0, 0))
    return pl.pallas_call(
        _out_proj_kernel,
        grid=(n // tm,),
        in_specs=[row(), heads(), heads(), mod(0), mod(1), mod(2), mod(3), mod(4),
                  _const_spec((1, d)), _const_spec((1, d)), _const_spec((d, 2 * d)),
                  _const_spec((1, 2 * d)), _const_spec((d, d)),
                  _const_spec((2 * N_EXPERTS, d))],
        out_specs=[row(), pl.BlockSpec((N_EXPERTS, tm), lambda i: (0, i))],
        out_shape=[jax.ShapeDtypeStruct((n, d), F32), jax.ShapeDtypeStruct((N_EXPERTS, n), F32)],
        compiler_params=_params(("arbitrary",)),
        name="out_proj",
    )(x, oa, ob, modr, modr, modr, modr, modr, lw["g_attn"], lw["g_ffn"], lw["w_gl"], lw["b_gate"],
      lw["w_out"], lw["wr"])


def _lane_cumsum(x_bf16, tri):
    t = x_bf16.shape[1]
    blk = tri.shape[0]
    parts = []
    carry = jnp.zeros((x_bf16.shape[0], 1), F32)
    for c in range(t // blk):
        part = _dot(x_bf16[:, c * blk:(c + 1) * blk], tri) + carry
        carry = part[:, blk - 1:blk]
        parts.append(part)
    return parts[0] if len(parts) == 1 else jnp.concatenate(parts, axis=1)


def _router_kernel(cap, slot_base, aff_ref, idx_ref, w_ref, pos_ref):
    r = pl.program_id(0)
    a = aff_ref[...]
    n_e, t = a.shape
    bits = pltpu.bitcast(a, jnp.int32)

    thr = jnp.zeros((n_e, 1), jnp.int32)
    for b in range(30, -1, -1):
        cand = thr | (1 << b)
        cnt = jnp.sum((bits >= cand).astype(jnp.int32), axis=1, keepdims=True)
        thr = jnp.where(cnt >= cap, cand, thr)
    gt = bits > thr
    eq = bits == thr
    need = (cap - jnp.sum(gt.astype(jnp.int32), axis=1, keepdims=True)).astype(F32)

    blk = min(MXU_DIM, t)
    ri = lax.broadcasted_iota(jnp.int32, (blk, blk), 0)
    ci = lax.broadcasted_iota(jnp.int32, (blk, blk), 1)
    tri = (ri <= ci).astype(BF16)
    cum_eq = _lane_cumsum(eq.astype(BF16), tri)
    sel = gt | (eq & (cum_eq <= need))
    cnt_incl = _lane_cumsum(sel.astype(BF16), tri)
    pad = jnp.zeros((LANES - n_e, t), F32)
    w_ref[...] = jnp.concatenate([jnp.where(sel, a, 0.0), pad], axis=0).T
    pos = cnt_incl - sel.astype(F32)
    pos_ref[...] = jnp.concatenate([pos, pad], axis=0).T.astype(jnp.int32) + (slot_base + r * cap)

    ts = min(LANES, cap)
    lane = lax.broadcasted_iota(jnp.int32, (1, LANES), 1)
    for sb in range(cap // ts):
        s_iota = (lax.broadcasted_iota(jnp.int32, (ts, 1), 0) + sb * ts).astype(F32)
        out = jnp.zeros((ts, LANES), F32)
        for e in range(n_e):
            below = jnp.where(cnt_incl[e:e + 1, :] <= s_iota, 1.0, 0.0)
            out = jnp.where(lane == e, jnp.sum(below, axis=1, keepdims=True), out)
        idx_ref[sb * ts:(sb + 1) * ts, :] = out.astype(jnp.int32) + r * t


def _router(aff_t, n_req, slot_base):
    n_e, n = aff_t.shape
    t = n // n_req
    cap = EC_FACTOR * t // N_EXPERTS
    return pl.pallas_call(
        functools.partial(_router_kernel, cap, slot_base),
        grid=(n_req,),
        in_specs=[pl.BlockSpec((n_e, t), lambda r: (0, r))],
        out_specs=[pl.BlockSpec((cap, LANES), lambda r: (r, 0)),
                   pl.BlockSpec((t, LANES), lambda r: (r, 0)),
                   pl.BlockSpec((t, LANES), lambda r: (r, 0))],
        out_shape=[jax.ShapeDtypeStruct((n_req * cap, LANES), jnp.int32),
                   jax.ShapeDtypeStruct((n, LANES), F32),
                   jax.ShapeDtypeStruct((n, LANES), jnp.int32)],
        compiler_params=_params(("arbitrary",)),
        name="router",
    )(aff_t)


def _ffn_kernel(ts, idx_ref, xp_hbm, xs_hbm, sh_ref, sc_ref, gn_ref, wg_ref, wu_ref, wd_ref, ye_ref,
                xbuf, sem):
    n_tiles = pl.num_programs(1)
    step = pl.program_id(0) * n_tiles + pl.program_id(1)
    last = pl.num_programs(0) * n_tiles - 1
    slot = step % 2
    nslot = 1 - slot
    d = xbuf.shape[2]

    def row_copy(src, tile, s, buf):
        return pltpu.make_async_copy(src.at[pl.ds(idx_ref[tile * ts + s], 1), :],
                                     xbuf.at[buf, pl.ds(s, 1), :], sem.at[buf])

    def wait_tile(buf):
        pltpu.make_async_copy(xs_hbm.at[pl.ds(0, ts), :], xbuf.at[buf], sem.at[buf]).wait()

    @pl.when(step == 0)
    def _():
        def body(s, carry):
            row_copy(xp_hbm, 0, s, 0).start()
            return carry
        lax.fori_loop(0, ts, body, 0, unroll=8)

    wait_tile(slot)
    nxt = jnp.minimum(step + 1, last)
    n_groups = FFN_GATE_CHUNKS + FFN_DOWN_CHUNKS
    per_group = ts // n_groups

    def run(src):
        issued = [0]

        def issue_group():
            for s in range(issued[0], issued[0] + per_group):
                row_copy(src, nxt, s, nslot).start()
            issued[0] += per_group

        h = (_rms(xbuf[slot]) * gn_ref[...] * (1.0 + sc_ref[...]) + sh_ref[...]).astype(BF16)
        fw = D_FF // FFN_GATE_CHUNKS
        hid = []
        for c in range(FFN_GATE_CHUNKS):
            issue_group()
            gate = _dot(h, wg_ref[:, c * fw:(c + 1) * fw])
            up = _dot(h, wu_ref[:, c * fw:(c + 1) * fw])
            hid.append((gate / (1.0 + jnp.exp(-gate)) * up).astype(BF16))
        hid = jnp.concatenate(hid, axis=1)
        dw = d // FFN_DOWN_CHUNKS
        for c in range(FFN_DOWN_CHUNKS):
            issue_group()
            ye_ref[:, c * dw:(c + 1) * dw] = _dot(hid, wd_ref[:, c * dw:(c + 1) * dw]).astype(BF16)

    next_is_context = nxt % n_tiles == 0

    @pl.when(next_is_context)
    def _():
        run(xp_hbm)

    @pl.when(jnp.logical_not(next_is_context))
    def _():
        run(xs_hbm)

    @pl.when(step == last)
    def _():
        wait_tile(nslot)


def _ffn(idx, x1p, x1s, modr, layer, lw, n_tiles, ts):
    d = x1p.shape[1]
    assert ts % (FFN_GATE_CHUNKS + FFN_DOWN_CHUNKS) == 0
    assert D_FF % FFN_GATE_CHUNKS == 0 and d % FFN_DOWN_CHUNKS == 0
    mod = lambda c: pl.BlockSpec((None, 1, d), lambda e, j, *_: ((layer * 8 + j) * 6 + c, 0, 0))
    grid_spec = pltpu.PrefetchScalarGridSpec(
        num_scalar_prefetch=1,
        grid=(N_EXPERTS, n_tiles),
        in_specs=[
            pl.BlockSpec(memory_space=pl.ANY), pl.BlockSpec(memory_space=pl.ANY), mod(3), mod(4),
            pl.BlockSpec((1, d), lambda e, j, *_: (0, 0)),
            pl.BlockSpec((None, None, d, D_FF), lambda e, j, *_: (layer, e, 0, 0)),
            pl.BlockSpec((None, None, d, D_FF), lambda e, j, *_: (layer, e, 0, 0)),
            pl.BlockSpec((None, None, D_FF, d), lambda e, j, *_: (layer, e, 0, 0)),
        ],
        out_specs=pl.BlockSpec((None, ts, d), lambda e, j, *_: (e, j, 0)),
        scratch_shapes=[pltpu.VMEM((2, ts, d), F32), pltpu.SemaphoreType.DMA((2,))],
    )
    return pl.pallas_call(
        functools.partial(_ffn_kernel, ts),
        grid_spec=grid_spec,
        out_shape=jax.ShapeDtypeStruct((N_EXPERTS, n_tiles * ts, d), BF16),
        compiler_params=_params(("arbitrary", "arbitrary")),
        name="ffn",
    )(idx, x1p, x1s, modr, modr, lw["g_ffn"], lw["w_gate"], lw["w_up"], lw["w_down"])


def _combine_kernel(final, tm, slots, starts_ref, ye_hbm, x_ref, pos_ref, w_ref, gt_ref, gf_ref, o_ref,
                    stage, stage_x, acc_ref, sem, sem_x):
    i = pl.program_id(0)
    last = pl.num_programs(0) - 1
    n_e, wn = N_EXPERTS, COMBINE_W
    per_pass = MXU_DIM // wn
    top = n_e * slots - wn
    slot = i % 2
    nslot = 1 - slot

    def window(tile, e, c):
        s_lo = starts_ref[tile * n_e + e]
        nominal = e * slots + (s_lo // BF16_ROWS) * BF16_ROWS + c * wn
        return jnp.minimum(nominal, top), nominal

    def win_copy(tile, e, c, dst, s):
        base, _ = window(tile, e, c)
        return pltpu.make_async_copy(ye_hbm.at[pl.ds(pl.multiple_of(base, BF16_ROWS), wn), :],
                                     dst.at[pl.ds(e * wn, wn), :], s)

    @pl.when(i == 0)
    def _():
        for e in range(n_e):
            win_copy(0, e, 0, stage.at[0], sem.at[0]).start()

    pltpu.make_async_copy(ye_hbm.at[pl.ds(0, n_e * wn), :], stage.at[slot], sem.at[slot]).wait()
    nxt = jnp.minimum(i + 1, last)
    for e in range(n_e):
        win_copy(nxt, e, 0, stage.at[nslot], sem.at[nslot]).start()

    pos = pos_ref[...]
    w = w_ref[...]
    lane = lax.broadcasted_iota(jnp.int32, (1, MXU_DIM), 1)

    def accumulate(src, c, first):
        for g in range(n_e // per_pass):
            tgt = val = None
            for j in range(per_pass):
                e = g * per_pass + j
                base, nominal = window(i, e, c)
                gs = pos[:, e:e + 1] + e * slots
                t_j = gs - (base - j * wn)
                in_round = (gs >= nominal) & (gs < nominal + wn)
                v_j = jnp.where(in_round, w[:, e:e + 1], 0.0)
                if j == 0:
                    tgt, val = t_j, v_j
                else:
                    here = lane >= j * wn
                    tgt = jnp.where(here, t_j, tgt)
                    val = jnp.where(here, v_j, val)
            place = jnp.where(tgt == lane, val, 0.0).astype(BF16)
            part = _dot(place, src[g * MXU_DIM:(g + 1) * MXU_DIM, :])
            if first and g == 0:
                acc_ref[...] = part
            else:
                acc_ref[...] += part

    accumulate(stage.at[slot], 0, True)

    rounds = 1
    for e in range(n_e):
        s_lo = starts_ref[i * n_e + e]
        span = starts_ref[(i + 1) * n_e + e] - (s_lo // BF16_ROWS) * BF16_ROWS
        rounds = jnp.maximum(rounds, (span + wn - 1) // wn)

    def extra_round(c, carry):
        for e in range(n_e):
            win_copy(i, e, c, stage_x, sem_x).start()
        pltpu.make_async_copy(ye_hbm.at[pl.ds(0, n_e * wn), :], stage_x, sem_x).wait()
        accumulate(stage_x, c, False)
        return carry

    lax.fori_loop(1, rounds, extra_round, 0)

    y = x_ref[...] + gt_ref[...] * acc_ref[...]
    if final:
        y = _rms(y) * gf_ref[...]
    o_ref[...] = y

    @pl.when(i == last)
    def _():
        pltpu.make_async_copy(ye_hbm.at[pl.ds(0, n_e * wn), :], stage.at[nslot], sem.at[nslot]).wait()


def _combine(starts, ye, x1, pos, w, modr, mod_row, g_final, *, slots, final):
    n, d = x1.shape
    tm = COMBINE_TM
    assert MXU_DIM % COMBINE_W == 0 and N_EXPERTS % (MXU_DIM // COMBINE_W) == 0
    grid_spec = pltpu.PrefetchScalarGridSpec(
        num_scalar_prefetch=1,
        grid=(n // tm,),
        in_specs=[
            pl.BlockSpec(memory_space=pl.ANY),
            pl.BlockSpec((tm, d), lambda i, *_: (i, 0)),
            pl.BlockSpec((tm, LANES), lambda i, *_: (i, 0)),
            pl.BlockSpec((tm, LANES), lambda i, *_: (i, 0)),
            pl.BlockSpec((None, 1, d), lambda i, *_: (mod_row(i * tm) * 6 + 5, 0, 0)),
            pl.BlockSpec((1, d), lambda i, *_: (0, 0)),
        ],
        out_specs=pl.BlockSpec((tm, d), lambda i, *_: (i, 0)),
        scratch_shapes=[pltpu.VMEM((2, N_EXPERTS * COMBINE_W, d), BF16),
                        pltpu.VMEM((N_EXPERTS * COMBINE_W, d), BF16),
                        pltpu.VMEM((tm, d), F32),
                        pltpu.SemaphoreType.DMA((2,)), pltpu.SemaphoreType.DMA(())],
    )
    return pl.pallas_call(
        functools.partial(_combine_kernel, final, tm, slots),
        grid_spec=grid_spec,
        out_shape=jax.ShapeDtypeStruct((n, d), F32),
        compiler_params=_params(("arbitrary",)),
        name="combine_final" if final else "combine",
    )(starts, ye, x1, pos, w, modr, g_final)


def _rope_tables(n_tokens, rot_dim):
    t = jnp.arange(n_tokens)
    row = (t // GRID_W).astype(F32)
    col = (t % GRID_W).astype(F32)
    quarter = rot_dim // 4
    inv = ROPE_THETA ** (-jnp.arange(quarter, dtype=F32) / quarter)
    ar, ac = row[:, None] * inv, col[:, None] * inv
    cos = jnp.concatenate([jnp.cos(ar), jnp.cos(ar), jnp.cos(ac), jnp.cos(ac)], axis=-1)
    sin = jnp.concatenate([-jnp.sin(ar), jnp.sin(ar), -jnp.sin(ac), jnp.sin(ac)], axis=-1)
    pad = LANES - rot_dim
    if pad:
        cos = jnp.concatenate([cos, jnp.ones((n_tokens, pad), F32)], axis=-1)
        sin = jnp.concatenate([sin, jnp.zeros((n_tokens, pad), F32)], axis=-1)
    return cos, sin


def _cast_kernel(*refs):
    n = len(refs) // 2
    for src, dst in zip(refs[:n], refs[n:]):
        dst[...] = src[...].astype(BF16)


def _cast_experts(*ws):
    n_l, n_e = ws[0].shape[:2]
    spec = lambda w: pl.BlockSpec((None, None, w.shape[2] // 2, w.shape[3]),
                                  lambda i, h: (i // n_e, i % n_e, h, 0))
    return pl.pallas_call(
        _cast_kernel,
        grid=(n_l * n_e, 2),
        in_specs=[spec(w) for w in ws],
        out_specs=[spec(w) for w in ws],
        out_shape=[jax.ShapeDtypeStruct(w.shape, BF16) for w in ws],
        compiler_params=_params(("arbitrary", "arbitrary")),
        name="cast_experts",
    )(*ws)


def _repack_kernel(o_kr, o_gl, w_ref, wa_ref, wgl_ref):
    rows = w_ref.shape[0]
    wa_ref[:, :C_KR] = w_ref[:, :C_KR].astype(BF16)
    wa_ref[:, C_KR:C_GQ] = jnp.concatenate(
        [w_ref[:, C_KR:o_kr], jnp.zeros((rows, C_GQ - C_KR - MLA_ROPE), F32)], axis=1).astype(BF16)
    wa_ref[:, C_GQ:] = w_ref[:, o_kr:o_gl].astype(BF16)
    wgl_ref[...] = w_ref[:, o_gl:].astype(BF16)


def _repack_w_in(w_in, l):
    _, d, cols = w_in.shape
    o_kr = MLA_Q_RANK + MLA_KV_RANK + MLA_ROPE
    o_gl = o_kr + (GQA_HEADS + 2 * GQA_KV_HEADS) * GQA_HEAD_DIM
    tm = 256
    return pl.pallas_call(
        functools.partial(_repack_kernel, o_kr, o_gl),
        grid=(d // tm,),
        in_specs=[pl.BlockSpec((None, tm, cols), lambda i: (l, i, 0))],
        out_specs=[pl.BlockSpec((tm, C_END), lambda i: (i, 0)),
                   pl.BlockSpec((tm, cols - o_gl), lambda i: (i, 0))],
        out_shape=[jax.ShapeDtypeStruct((d, C_END), BF16), jax.ShapeDtypeStruct((d, cols - o_gl), BF16)],
        compiler_params=_params(("arbitrary",)),
        name="repack_w_in",
    )(w_in)


def _layer_weights(l, g_attn_norm, w_in, b_gate, g_mla_q, w_mla_uq, g_mla_kv, w_mla_uk, w_mla_uv,
                   g_gqa_q, g_gqa_k, w_out, g_ffn_norm, w_router, w_gate_e, w_up_e, w_down_e):
    w_a, w_gl = _repack_w_in(w_in, l)
    w_uq = w_mla_uq[l].reshape(MLA_Q_RANK, MLA_HEADS, MLA_QK)
    w_uq = jnp.pad(w_uq, ((0, 0), (0, 0), (0, MLA_QK_PAD - MLA_QK))).reshape(MLA_Q_RANK, -1).astype(BF16)
    wr = w_router[l].T
    wr_hi = wr.astype(BF16)
    wr_lo = (wr - wr_hi.astype(F32)).astype(BF16)
    return {
        "g_attn": g_attn_norm[l][None], "g_ffn": g_ffn_norm[l][None], "w_a": w_a,
        "g_mla_q": g_mla_q[l][None], "w_uq": w_uq, "g_mla_kv": g_mla_kv[l][None],
        "g_gqa_q": g_gqa_q[l][None], "g_gqa_k": g_gqa_k[l][None],
        "w_uk": w_mla_uk[l].astype(BF16), "w_uv": w_mla_uv[l].astype(BF16),
        "w_gl": w_gl, "b_gate": b_gate[l][None], "w_out": w_out[l].astype(BF16),
        "wr": jnp.concatenate([wr_hi, wr_lo], axis=0),
        "w_gate": w_gate_e, "w_up": w_up_e, "w_down": w_down_e,
    }


def _combine_starts(pos, slot_end):
    starts = jnp.concatenate([pos[::COMBINE_TM, :N_EXPERTS],
                              jnp.full((1, N_EXPERTS), slot_end, jnp.int32)], axis=0)
    return starts.reshape(-1)


def kernel(x_prompt, x_sample, c, cache_mla_ckv, cache_mla_krope, cache_gqa_k, cache_gqa_v, c_ctx,
           w_ada, b_ada, g_attn_norm, w_in, b_gate, g_mla_q, w_mla_uq, g_mla_kv, w_mla_uk, w_mla_uv,
           g_gqa_q, g_gqa_k, w_out, g_ffn_norm, w_router, w_gate_e, w_up_e, w_down_e, g_final):
    bp, tp, d = x_prompt.shape
    bs, tl, _ = x_sample.shape
    n_layers = w_ada.shape[0]
    np_, ns = bp * tp, bs * tl
    kvw = GQA_KV_HEADS * GQA_HEAD_DIM
    cap_p = EC_FACTOR * tp // N_EXPERTS
    cap_s = EC_FACTOR * tl // N_EXPERTS
    ts = cap_s
    assert bp * cap_p == ts and bs + 1 <= 8
    n_tiles = 1 + bs
    slots = n_tiles * ts

    cc = jnp.zeros((8, d), F32).at[0].set(c_ctx).at[1:1 + bs].set(c)
    modr = _ada(cc, w_ada, b_ada).reshape(n_layers * 8 * 6, 1, d)

    cos_m, sin_m = _rope_tables(tl, MLA_ROPE)
    cos_g, sin_g = _rope_tables(tl, GQA_HEAD_DIM)
    tabs = (cos_m, sin_m, cos_g, sin_g)
    g_fin = g_final[None]

    w_gate_e, w_up_e, w_down_e = _cast_experts(w_gate_e, w_up_e, w_down_e)
    xp = x_prompt.reshape(np_, d)
    xs = x_sample.reshape(ns, d)
    st = {"ckv": [], "kr": [], "k": [], "v": []}
    for l in range(n_layers):
        lw = _layer_weights(l, g_attn_norm, w_in, b_gate, g_mla_q, w_mla_uq, g_mla_kv, w_mla_uk,
                            w_mla_uv, g_gqa_q, g_gqa_k, w_out, g_ffn_norm, w_router, w_gate_e,
                            w_up_e, w_down_e)
        row_p = lambda r, l=l: l * 8
        row_s = lambda r, l=l: l * 8 + 1 + r // tl

        qm, ckv, kr, qg, kg, vg, ckv32, kr32, kg32, vg32 = _qkv(
            xp, tp, modr, row_p, lw, tabs, rope=False, cache_out=True)
        st["ckv"].append(ckv32.reshape(bp, tp, MLA_KV_RANK))
        st["kr"].append(kr32.reshape(bp, tp, MLA_ROPE))
        st["k"].append(kg32.reshape(bp, tp, GQA_KV_HEADS, GQA_HEAD_DIM))
        st["v"].append(vg32.reshape(bp, tp, GQA_KV_HEADS, GQA_HEAD_DIM))
        k_m, v_m = _kv_expand(ckv.reshape(bp, tp, -1), kr.reshape(bp, tp, -1), lw["w_uk"], lw["w_uv"])
        oa_p = _attn(qm, k_m, v_m, hps=MLA_HEADS, rolled=False)
        ob_p = _attn(qg.reshape(bp, GQA_KV_HEADS, GQA_GROUP * tp, GQA_HEAD_DIM), kg.reshape(bp, tp, -1),
                     vg.reshape(bp, tp, -1), hps=GQA_KV_HEADS, rolled=False)
        x1p, aff_p = _out_proj(xp, tp, oa_p, ob_p.reshape(bp, GQA_HEADS, tp, HEAD_V), modr, row_p, lw)

        qm, ckv, kr, qg, kg, vg = _qkv(xs, tl, modr, row_s, lw, tabs, rope=True, cache_out=False)
        kr_cache = jnp.pad(cache_mla_krope[:, l], ((0, 0), (0, 0), (0, LANES - MLA_ROPE))).astype(BF16)
        ckv_all = jnp.concatenate([cache_mla_ckv[:, l].astype(BF16), ckv.reshape(bs, tl, -1)], axis=1)
        kr_all = jnp.concatenate([kr_cache, kr.reshape(bs, tl, -1)], axis=1)
        k_m, v_m = _kv_expand(ckv_all, kr_all, lw["w_uk"], lw["w_uv"])
        oa_s = _attn(qm, k_m, v_m, hps=1, rolled=True)
        past = cache_gqa_k.shape[2]
        k_all = jnp.concatenate([cache_gqa_k[:, l].reshape(bs, past, kvw).astype(BF16),
                                 kg.reshape(bs, tl, kvw)], axis=1)
        v_all = jnp.concatenate([cache_gqa_v[:, l].reshape(bs, past, kvw).astype(BF16),
                                 vg.reshape(bs, tl, kvw)], axis=1)
        ob_s = _attn(qg.reshape(bs, GQA_KV_HEADS, GQA_GROUP * tl, GQA_HEAD_DIM), k_all, v_all,
                     hps=1, rolled=True)
        x1s, aff_s = _out_proj(xs, tl, oa_s, ob_s.reshape(bs, GQA_HEADS, tl, HEAD_V), modr, row_s, lw)

        idx_p, w_p, pos_p = _router(aff_p, bp, 0)
        idx_s, w_s, pos_s = _router(aff_s, bs, ts)
        idx = jnp.concatenate([idx_p[:, :N_EXPERTS], idx_s[:, :N_EXPERTS]], axis=0).T.reshape(-1)
        ye = _ffn(idx, x1p, x1s, modr, l, lw, n_tiles, ts).reshape(N_EXPERTS * slots, d)
        final = l == n_layers - 1
        xp = _combine(_combine_starts(pos_p, ts), ye, x1p, pos_p, w_p, modr, row_p, g_fin,
                      slots=slots, final=final)
        xs = _combine(_combine_starts(pos_s, slots), ye, x1s, pos_s, w_s, modr, row_s, g_fin,
                      slots=slots, final=final)

    return (xp.reshape(bp, tp, d), xs.reshape(bs, tl, d),
            jnp.stack(st["ckv"], axis=1), jnp.stack(st["kr"], axis=1),
            jnp.stack(st["k"], axis=1), jnp.stack(st["v"], axis=1))
```

```python
import functools

import jax
import jax.numpy as jnp
from jax import lax
from jax.experimental import pallas as pl
from jax.experimental.pallas import tpu as pltpu

F32 = jnp.float32
BF16 = jnp.bfloat16

D_MODEL = 2048
GRID_W = 64
ROPE_THETA = 10000.0
EPS = 1e-6
MLA_HEADS = 16
MLA_Q_RANK = 512
MLA_KV_RANK = 256
MLA_NOPE = 128
MLA_ROPE = 64
MLA_V = 128
MLA_QK = MLA_NOPE + MLA_ROPE
MLA_SCALE = MLA_QK ** -0.5
GQA_HEADS = 16
GQA_KV_HEADS = 4
GQA_GROUP = GQA_HEADS // GQA_KV_HEADS
GQA_HEAD_DIM = 128
GQA_SCALE = GQA_HEAD_DIM ** -0.5
HEAD_V = 128
N_EXPERTS = 16
EC_FACTOR = 2
D_FF = 1024
LOG2E = 1.4426950408889634

LANES = 128
BF16_ROWS = 16
MXU_DIM = 256
MLA_QK_PAD = MXU_DIM
VMEM_LIMIT = 56 * 1024 * 1024

ROW_TM = 256
ATTN_TQ = 512
ATTN_TK = 512
COMBINE_TM = 256
COMBINE_W = 64
FFN_GATE_CHUNKS = 4
FFN_DOWN_CHUNKS = 4

C_CQ = 0
C_CKV = C_CQ + MLA_Q_RANK
C_KR = C_CKV + MLA_KV_RANK
C_GQ = C_KR + LANES
C_GK = C_GQ + GQA_HEADS * GQA_HEAD_DIM
C_GV = C_GK + GQA_KV_HEADS * GQA_HEAD_DIM
C_END = C_GV + GQA_KV_HEADS * GQA_HEAD_DIM


def _dot(a, b):
    return jnp.dot(a, b, preferred_element_type=F32)


def _dot_nt(a, b):
    return lax.dot_general(a, b, (((1,), (1,)), ((), ())), preferred_element_type=F32)


def _split(x):
    hi = x.astype(BF16)
    lo = (x - hi.astype(F32)).astype(BF16)
    return hi, lo


def _rms(x):
    return x * lax.rsqrt(jnp.mean(x * x, axis=-1, keepdims=True) + EPS)


def _params(sem):
    return pltpu.CompilerParams(dimension_semantics=sem, vmem_limit_bytes=VMEM_LIMIT)


def _const_spec(shape):
    nd = len(shape)
    return pl.BlockSpec(shape, lambda *_: (0,) * nd, pipeline_mode=pl.Buffered(1))


def _ada_kernel(c_ref, w_ref, b_ref, o_ref):
    c = c_ref[...]
    s = c / (1.0 + jnp.exp(-c))
    rows = c.shape[0]
    s_hi, s_lo = _split(s)
    w_hi, w_lo = _split(w_ref[...])
    both = _dot(jnp.concatenate([s_hi, s_lo], axis=0), w_hi)
    o_ref[...] = both[:rows] + both[rows:] + _dot(s_hi, w_lo) + b_ref[...]


def _ada(cc, w_ada, b_ada):
    n_layers, d, n6 = w_ada.shape
    rows = cc.shape[0]
    tn = 1024
    return pl.pallas_call(
        _ada_kernel,
        grid=(n_layers, n6 // tn),
        in_specs=[
            pl.BlockSpec((rows, d), lambda l, j: (0, 0)),
            pl.BlockSpec((None, d, tn), lambda l, j: (l, 0, j)),
            pl.BlockSpec((None, 1, tn), lambda l, j: (l, 0, j)),
        ],
        out_specs=pl.BlockSpec((None, rows, tn), lambda l, j: (l, 0, j)),
        out_shape=jax.ShapeDtypeStruct((n_layers, rows, n6), F32),
        compiler_params=_params(("arbitrary", "arbitrary")),
        name="ada",
    )(cc, w_ada, b_ada.reshape(n_layers, 1, n6))


def _swap_halves(x, q):
    lane = lax.broadcasted_iota(jnp.int32, x.shape, 1)
    first = (lane % (2 * q)) < q
    return jnp.where(first, pltpu.roll(x, LANES - q, 1), pltpu.roll(x, q, 1))


def _qkv_kernel(rope, cache_out, x_ref, sh_ref, sc_ref, gn_ref, w_ref, gq_ref, wuq_ref, gkv_ref,
                ggq_ref, ggk_ref, cm_ref, sm_ref, cg_ref, sg_ref, *outs):
    qm_ref, ckv_ref, kr_ref, qg_ref, kg_ref, vg_ref = outs[:6]
    x = x_ref[...]
    h = (_rms(x) * gn_ref[...] * (1.0 + sc_ref[...]) + sh_ref[...]).astype(BF16)

    cq = _rms(_dot(h, w_ref[:, C_CQ:C_CKV])) * gq_ref[...]
    qm = _dot(cq.astype(BF16), wuq_ref[...]) * (MLA_SCALE * LOG2E)
    for hd in range(MLA_HEADS):
        lo = hd * MLA_QK_PAD
        qm_ref[hd, :, :LANES] = qm[:, lo:lo + LANES].astype(BF16)
        t = qm[:, lo + LANES:lo + 2 * LANES]
        if rope:
            t = t * cm_ref[...] + _swap_halves(t, MLA_ROPE // 4) * sm_ref[...]
        qm_ref[hd, :, LANES:] = t.astype(BF16)

    ckv = _rms(_dot(h, w_ref[:, C_CKV:C_KR])) * gkv_ref[...]
    ckv_ref[...] = ckv.astype(BF16)
    kr = _dot(h, w_ref[:, C_KR:C_GQ])
    if cache_out:
        outs[6][...] = ckv
        outs[7][...] = kr[:, :MLA_ROPE]
    if rope:
        kr = kr * cm_ref[...] + _swap_halves(kr, MLA_ROPE // 4) * sm_ref[...]
    kr_ref[...] = kr.astype(BF16)

    zq = _dot(h, w_ref[:, C_GQ:C_GK])
    for hd in range(GQA_HEADS):
        lo = hd * GQA_HEAD_DIM
        t = _rms(zq[:, lo:lo + GQA_HEAD_DIM]) * ggq_ref[...]
        if rope:
            t = t * cg_ref[...] + _swap_halves(t, GQA_HEAD_DIM // 4) * sg_ref[...]
        qg_ref[hd] = (t * (GQA_SCALE * LOG2E)).astype(BF16)
    zk = _dot(h, w_ref[:, C_GK:C_GV])
    for hd in range(GQA_KV_HEADS):
        lo = hd * GQA_HEAD_DIM
        t = _rms(zk[:, lo:lo + GQA_HEAD_DIM]) * ggk_ref[...]
        if cache_out:
            outs[8][:, lo:lo + GQA_HEAD_DIM] = t
        if rope:
            t = t * cg_ref[...] + _swap_halves(t, GQA_HEAD_DIM // 4) * sg_ref[...]
        kg_ref[:, lo:lo + GQA_HEAD_DIM] = t.astype(BF16)
    zv = _dot(h, w_ref[:, C_GV:C_END])
    vg_ref[...] = zv.astype(BF16)
    if cache_out:
        outs[9][...] = zv


def _qkv(x, t_len, modr, mod_row, lw, tabs, *, rope, cache_out):
    n, d = x.shape
    b = n // t_len
    tm = min(ROW_TM, t_len)
    tpb = t_len // tm
    pos_tiles = tabs[0].shape[0] // tm
    kvw = GQA_KV_HEADS * GQA_HEAD_DIM
    row = lambda w: pl.BlockSpec((tm, w), lambda i: (i, 0))
    row_in = row(d)
    heads = lambda nh, w: pl.BlockSpec((None, nh, tm, w), lambda i: (i // tpb, 0, i % tpb, 0))
    mod = lambda j: pl.BlockSpec((None, 1, d), lambda i: (mod_row(i * tm) * 6 + j, 0, 0))
    tab = lambda w: pl.BlockSpec((tm, w), lambda i: (i % pos_tiles, 0))
    in_specs = [
        row_in, mod(0), mod(1), _const_spec((1, d)), _const_spec((d, C_END)),
        _const_spec((1, MLA_Q_RANK)), _const_spec((MLA_Q_RANK, MLA_HEADS * MLA_QK_PAD)),
        _const_spec((1, MLA_KV_RANK)), _const_spec((1, GQA_HEAD_DIM)), _const_spec((1, GQA_HEAD_DIM)),
        tab(LANES), tab(LANES), tab(LANES), tab(LANES),
    ]
    out_shape = [
        jax.ShapeDtypeStruct((b, MLA_HEADS, t_len, MLA_QK_PAD), BF16),
        jax.ShapeDtypeStruct((n, MLA_KV_RANK), BF16),
        jax.ShapeDtypeStruct((n, LANES), BF16),
        jax.ShapeDtypeStruct((b, GQA_HEADS, t_len, GQA_HEAD_DIM), BF16),
        jax.ShapeDtypeStruct((n, kvw), BF16),
        jax.ShapeDtypeStruct((n, kvw), BF16),
    ]
    out_specs = [heads(MLA_HEADS, MLA_QK_PAD), row(MLA_KV_RANK), row(LANES),
                 heads(GQA_HEADS, GQA_HEAD_DIM), row(kvw), row(kvw)]
    if cache_out:
        out_shape += [
            jax.ShapeDtypeStruct((n, MLA_KV_RANK), F32),
            jax.ShapeDtypeStruct((n, MLA_ROPE), F32),
            jax.ShapeDtypeStruct((n, kvw), F32),
            jax.ShapeDtypeStruct((n, kvw), F32),
        ]
        out_specs += [row(MLA_KV_RANK), row(MLA_ROPE), row(kvw), row(kvw)]
    return pl.pallas_call(
        functools.partial(_qkv_kernel, rope, cache_out),
        grid=(n // tm,),
        in_specs=in_specs,
        out_specs=out_specs,
        out_shape=out_shape,
        compiler_params=_params(("arbitrary",)),
        name="qkv_rope" if rope else "qkv_ctx",
    )(x, modr, modr, lw["g_attn"], lw["w_a"], lw["g_mla_q"], lw["w_uq"], lw["g_mla_kv"],
      lw["g_gqa_q"], lw["g_gqa_k"], *tabs)


def _kv_expand_kernel(ckv_ref, kr_ref, wuk_ref, wuv_ref, k_ref, v_ref):
    ckv = ckv_ref[...]
    kn = _dot(ckv, wuk_ref[...]).astype(BF16)
    kr = kr_ref[...]
    for hd in range(MLA_HEADS):
        k_ref[:, hd * MLA_QK_PAD:hd * MLA_QK_PAD + LANES] = kn[:, hd * MLA_NOPE:(hd + 1) * MLA_NOPE]
        k_ref[:, hd * MLA_QK_PAD + LANES:(hd + 1) * MLA_QK_PAD] = kr
    v_ref[...] = _dot(ckv, wuv_ref[...]).astype(BF16)


def _kv_expand(ckv, kr, w_uk, w_uv):
    b, s, _ = ckv.shape
    ts = min(512, s)
    kw, vw = MLA_HEADS * MLA_QK_PAD, MLA_HEADS * MLA_V
    blk = lambda w: pl.BlockSpec((None, ts, w), lambda i, j: (i, j, 0))
    return pl.pallas_call(
        _kv_expand_kernel,
        grid=(b, s // ts),
        in_specs=[blk(MLA_KV_RANK), blk(LANES), _const_spec((MLA_KV_RANK, MLA_HEADS * MLA_NOPE)),
                  _const_spec((MLA_KV_RANK, vw))],
        out_specs=[blk(kw), blk(vw)],
        out_shape=[jax.ShapeDtypeStruct((b, s, kw), BF16), jax.ShapeDtypeStruct((b, s, vw), BF16)],
        compiler_params=_params(("arbitrary", "arbitrary")),
        name="kv_expand",
    )(ckv, kr, w_uk, w_uv)


def _attn_kernel(tq, tk, dq, hps, rolled, q_ref, k_ref, v_ref, o_ref, s0, s1, m0, m1, vx_ref):
    s_len = k_ref.shape[0]
    rows = q_ref.shape[1]
    n_chunks = s_len // tk
    dv = HEAD_V
    bufs = ((s0, m0), (s1, m1))

    for hh in range(hps):
        vx_ref[hh, :, :dv] = v_ref[:, hh * dv:(hh + 1) * dv]
        vx_ref[hh, :, dv:] = jnp.ones((s_len, dv), BF16)

    def stage(p1, p2):
        if p1 is not None:
            h1, r1, b1 = p1
            q = q_ref[h1, pl.ds(r1, tq), :]
            sb1, mb1 = bufs[b1]
        if p2 is not None:
            h2, r2, b2 = p2
            sb2, mb2 = bufs[b2]
            m = jnp.max(mb2[...], axis=1, keepdims=True)
            acc = None
        for c in range(n_chunks):
            ck = slice(c * tk, (c + 1) * tk)
            if p1 is not None:
                s = _dot_nt(q, k_ref[ck, h1 * dq:(h1 + 1) * dq])
                sb1[:, ck] = s
                mx = s[:, :LANES]
                for t in range(1, tk // LANES):
                    mx = jnp.maximum(mx, s[:, t * LANES:(t + 1) * LANES])
                mb1[...] = mx if c == 0 else jnp.maximum(mb1[...], mx)
            if p2 is not None:
                p = jnp.exp2(sb2[:, ck] - m).astype(BF16)
                part = _dot(p, vx_ref[h2, ck, :])
                acc = part if acc is None else acc + part
        if p2 is not None:
            o_ref[h2, pl.ds(r2, tq), :] = (acc[:, :dv] / acc[:, dv:]).astype(BF16)

    n_units = rows // tq
    if rolled:
        stage((0, 0, 0), None)

        def body(i, carry):
            u = 2 * i
            stage((0, pl.multiple_of((u + 1) * tq, tq), 1), (0, pl.multiple_of(u * tq, tq), 0))
            nxt = jnp.minimum(u + 2, n_units - 1)
            stage((0, pl.multiple_of(nxt * tq, tq), 0), (0, pl.multiple_of((u + 1) * tq, tq), 1))
            return carry

        lax.fori_loop(0, n_units // 2, body, 0)
    else:
        units = [(hh, u * tq) for hh in range(hps) for u in range(n_units)]
        for j in range(len(units) + 1):
            p1 = units[j] + (j % 2,) if j < len(units) else None
            p2 = units[j - 1] + ((j - 1) % 2,) if j > 0 else None
            stage(p1, p2)


def _attn(q, k, v, *, hps, rolled):
    b, hk, rows, dq = q.shape
    s = k.shape[1]
    dv = HEAD_V
    tq = min(ATTN_TQ, rows)
    tk = min(ATTN_TK, s)
    assert rows % tq == 0 and s % tk == 0 and hk % hps == 0
    assert not rolled or (hps == 1 and (rows // tq) % 2 == 0)
    return pl.pallas_call(
        functools.partial(_attn_kernel, tq, tk, dq, hps, rolled),
        grid=(b, hk // hps),
        in_specs=[
            pl.BlockSpec((None, hps, rows, dq), lambda i, h: (i, h, 0, 0)),
            pl.BlockSpec((None, s, hps * dq), lambda i, h: (i, 0, h)),
            pl.BlockSpec((None, s, hps * dv), lambda i, h: (i, 0, h)),
        ],
        out_specs=pl.BlockSpec((None, hps, rows, dv), lambda i, h: (i, h, 0, 0)),
        out_shape=jax.ShapeDtypeStruct((b, hk, rows, dv), BF16),
        scratch_shapes=[pltpu.VMEM((tq, s), F32), pltpu.VMEM((tq, s), F32),
                        pltpu.VMEM((tq, LANES), F32), pltpu.VMEM((tq, LANES), F32),
                        pltpu.VMEM((hps, s, 2 * dv), BF16)],
        compiler_params=_params(("arbitrary", "arbitrary")),
        name="attn_rolled" if rolled else "attn_flat",
    )(q, k, v)


def _out_proj_kernel(x_ref, oa_ref, ob_ref, sh1_ref, sc1_ref, gt1_ref, sh2_ref, sc2_ref, gn1_ref,
                     gn2_ref, wgl_ref, bg_ref, wo_ref, wr_ref, x1_ref, aff_ref):
    d = x_ref.shape[1]
    x = x_ref[...]
    h = (_rms(x) * gn1_ref[...] * (1.0 + sc1_ref[...]) + sh1_ref[...]).astype(BF16)
    ga = 1.0 / (1.0 + jnp.exp(-(_dot(h, wgl_ref[:, :d]) + bg_ref[:, :d])))
    gb = 1.0 / (1.0 + jnp.exp(-(_dot(h, wgl_ref[:, d:]) + bg_ref[:, d:])))
    parts = []
    for hd in range(d // HEAD_V):
        cs = slice(hd * HEAD_V, (hd + 1) * HEAD_V)
        parts.append((ga[:, cs] * oa_ref[hd].astype(F32) + gb[:, cs] * ob_ref[hd].astype(F32)).astype(BF16))
    merged = jnp.concatenate(parts, axis=1)
    x1 = x + gt1_ref[...] * _dot(merged, wo_ref[...])
    x1_ref[...] = x1
    h2 = _rms(x1) * gn2_ref[...] * (1.0 + sc2_ref[...]) + sh2_ref[...]
    h_hi, h_lo = _split(h2)
    n_e = aff_ref.shape[0]
    both = _dot_nt(wr_ref[...], h_hi)
    lg = both[:n_e] + both[n_e:] + _dot_nt(wr_ref[:n_e, :], h_lo)
    e = jnp.exp(lg - jnp.max(lg, axis=0, keepdims=True))
    aff_ref[...] = e / jnp.sum(e, axis=0, keepdims=True)


def _out_proj(x, t_len, oa, ob, modr, mod_row, lw):
    n, d = x.shape
    tm = min(ROW_TM, t_len)
    tpb = t_len // tm
    nh = d // HEAD_V
    row = lambda: pl.BlockSpec((tm, d), lambda i: (i, 0))
    heads = lambda: pl.BlockSpec((None, nh, tm, HEAD_V), lambda i: (i // tpb, 0, i % tpb, 0))
    mod = lambda j: pl.BlockSpec((None, 1, d), lambda i: (mod_row(i * tm) * 6 + j, 0, 0))
    return pl.pallas_call(
        _out_proj_kernel,
        grid=(n // tm,),
        in_specs=[row(), heads(), heads(), mod(0), mod(1), mod(2), mod(3), mod(4),
                  _const_spec((1, d)), _const_spec((1, d)), _const_spec((d, 2 * d)),
                  _const_spec((1, 2 * d)), _const_spec((d, d)),
                  _const_spec((2 * N_EXPERTS, d))],
        out_specs=[row(), pl.BlockSpec((N_EXPERTS, tm), lambda i: (0, i))],
        out_shape=[jax.ShapeDtypeStruct((n, d), F32), jax.ShapeDtypeStruct((N_EXPERTS, n), F32)],
        compiler_params=_params(("arbitrary",)),
        name="out_proj",
    )(x, oa, ob, modr, modr, modr, modr, modr, lw["g_attn"], lw["g_ffn"], lw["w_gl"], lw["b_gate"],
      lw["w_out"], lw["wr"])


def _lane_cumsum(x_bf16, tri):
    t = x_bf16.shape[1]
    blk = tri.shape[0]
    parts = []
    carry = jnp.zeros((x_bf16.shape[0], 1), F32)
    for c in range(t // blk):
        part = _dot(x_bf16[:, c * blk:(c + 1) * blk], tri) + carry
        carry = part[:, blk - 1:blk]
        parts.append(part)
    return parts[0] if len(parts) == 1 else jnp.concatenate(parts, axis=1)


def _router_kernel(cap, slot_base, aff_ref, idx_ref, w_ref, pos_ref):
    r = pl.program_id(0)
    a = aff_ref[...]
    n_e, t = a.shape
    bits = pltpu.bitcast(a, jnp.int32)

    thr = jnp.zeros((n_e, 1), jnp.int32)
    for b in range(30, -1, -1):
        cand = thr | (1 << b)
        cnt = jnp.sum((bits >= cand).astype(jnp.int32), axis=1, keepdims=True)
        thr = jnp.where(cnt >= cap, cand, thr)
    gt = bits > thr
    eq = bits == thr
    need = (cap - jnp.sum(gt.astype(jnp.int32), axis=1, keepdims=True)).astype(F32)

    blk = min(MXU_DIM, t)
    ri = lax.broadcasted_iota(jnp.int32, (blk, blk), 0)
    ci = lax.broadcasted_iota(jnp.int32, (blk, blk), 1)
    tri = (ri <= ci).astype(BF16)
    cum_eq = _lane_cumsum(eq.astype(BF16), tri)
    sel = gt | (eq & (cum_eq <= need))
    cnt_incl = _lane_cumsum(sel.astype(BF16), tri)
    pad = jnp.zeros((LANES - n_e, t), F32)
    w_ref[...] = jnp.concatenate([jnp.where(sel, a, 0.0), pad], axis=0).T
    pos = cnt_incl - sel.astype(F32)
    pos_ref[...] = jnp.concatenate([pos, pad], axis=0).T.astype(jnp.int32) + (slot_base + r * cap)

    ts = min(LANES, cap)
    lane = lax.broadcasted_iota(jnp.int32, (1, LANES), 1)
    for sb in range(cap // ts):
        s_iota = (lax.broadcasted_iota(jnp.int32, (ts, 1), 0) + sb * ts).astype(F32)
        out = jnp.zeros((ts, LANES), F32)
        for e in range(n_e):
            below = jnp.where(cnt_incl[e:e + 1, :] <= s_iota, 1.0, 0.0)
            out = jnp.where(lane == e, jnp.sum(below, axis=1, keepdims=True), out)
        idx_ref[sb * ts:(sb + 1) * ts, :] = out.astype(jnp.int32) + r * t


def _router(aff_t, n_req, slot_base):
    n_e, n = aff_t.shape
    t = n // n_req
    cap = EC_FACTOR * t // N_EXPERTS
    return pl.pallas_call(
        functools.partial(_router_kernel, cap, slot_base),
        grid=(n_req,),
        in_specs=[pl.BlockSpec((n_e, t), lambda r: (0, r))],
        out_specs=[pl.BlockSpec((cap, LANES), lambda r: (r, 0)),
                   pl.BlockSpec((t, LANES), lambda r: (r, 0)),
                   pl.BlockSpec((t, LANES), lambda r: (r, 0))],
        out_shape=[jax.ShapeDtypeStruct((n_req * cap, LANES), jnp.int32),
                   jax.ShapeDtypeStruct((n, LANES), F32),
                   jax.ShapeDtypeStruct((n, LANES), jnp.int32)],
        compiler_params=_params(("arbitrary",)),
        name="router",
    )(aff_t)


def _ffn_kernel(ts, idx_ref, xp_hbm, xs_hbm, sh_ref, sc_ref, gn_ref, wg_ref, wu_ref, wd_ref, ye_ref,
                xbuf, sem):
    n_tiles = pl.num_programs(1)
    step = pl.program_id(0) * n_tiles + pl.program_id(1)
    last = pl.num_programs(0) * n_tiles - 1
    slot = step % 2
    nslot = 1 - slot
    d = xbuf.shape[2]

    def row_copy(src, tile, s, buf):
        return pltpu.make_async_copy(src.at[pl.ds(idx_ref[tile * ts + s], 1), :],
                                     xbuf.at[buf, pl.ds(s, 1), :], sem.at[buf])

    def wait_tile(buf):
        pltpu.make_async_copy(xs_hbm.at[pl.ds(0, ts), :], xbuf.at[buf], sem.at[buf]).wait()

    @pl.when(step == 0)
    def _():
        def body(s, carry):
            row_copy(xp_hbm, 0, s, 0).start()
            return carry
        lax.fori_loop(0, ts, body, 0, unroll=8)

    wait_tile(slot)
    nxt = jnp.minimum(step + 1, last)
    n_groups = FFN_GATE_CHUNKS + FFN_DOWN_CHUNKS
    per_group = ts // n_groups

    def run(src):
        issued = [0]

        def issue_group():
            for s in range(issued[0], issued[0] + per_group):
                row_copy(src, nxt, s, nslot).start(priority=1)
            issued[0] += per_group

        h = (_rms(xbuf[slot]) * gn_ref[...] * (1.0 + sc_ref[...]) + sh_ref[...]).astype(BF16)
        fw = D_FF // FFN_GATE_CHUNKS
        hid = []
        for c in range(FFN_GATE_CHUNKS):
            issue_group()
            gate = _dot(h, wg_ref[:, c * fw:(c + 1) * fw])
            up = _dot(h, wu_ref[:, c * fw:(c + 1) * fw])
            hid.append((gate / (1.0 + jnp.exp(-gate)) * up).astype(BF16))
        hid = jnp.concatenate(hid, axis=1)
        dw = d // FFN_DOWN_CHUNKS
        for c in range(FFN_DOWN_CHUNKS):
            issue_group()
            ye_ref[:, c * dw:(c + 1) * dw] = _dot(hid, wd_ref[:, c * dw:(c + 1) * dw]).astype(BF16)

    next_is_context = nxt % n_tiles == 0

    @pl.when(next_is_context)
    def _():
        run(xp_hbm)

    @pl.when(jnp.logical_not(next_is_context))
    def _():
        run(xs_hbm)

    @pl.when(step == last)
    def _():
        wait_tile(nslot)


def _ffn(idx, x1p, x1s, modr, layer, lw, n_tiles, ts):
    d = x1p.shape[1]
    assert ts % (FFN_GATE_CHUNKS + FFN_DOWN_CHUNKS) == 0
    assert D_FF % FFN_GATE_CHUNKS == 0 and d % FFN_DOWN_CHUNKS == 0
    mod = lambda c: pl.BlockSpec((None, 1, d), lambda e, j, *_: ((layer * 8 + j) * 6 + c, 0, 0))
    grid_spec = pltpu.PrefetchScalarGridSpec(
        num_scalar_prefetch=1,
        grid=(N_EXPERTS, n_tiles),
        in_specs=[
            pl.BlockSpec(memory_space=pl.ANY), pl.BlockSpec(memory_space=pl.ANY), mod(3), mod(4),
            pl.BlockSpec((1, d), lambda e, j, *_: (0, 0)),
            pl.BlockSpec((None, None, d, D_FF), lambda e, j, *_: (layer, e, 0, 0)),
            pl.BlockSpec((None, None, d, D_FF), lambda e, j, *_: (layer, e, 0, 0)),
            pl.BlockSpec((None, None, D_FF, d), lambda e, j, *_: (layer, e, 0, 0)),
        ],
        out_specs=pl.BlockSpec((None, ts, d), lambda e, j, *_: (e, j, 0)),
        scratch_shapes=[pltpu.VMEM((2, ts, d), F32), pltpu.SemaphoreType.DMA((2,))],
    )
    return pl.pallas_call(
        functools.partial(_ffn_kernel, ts),
        grid_spec=grid_spec,
        out_shape=jax.ShapeDtypeStruct((N_EXPERTS, n_tiles * ts, d), BF16),
        compiler_params=_params(("arbitrary", "arbitrary")),
        name="ffn",
    )(idx, x1p, x1s, modr, modr, lw["g_ffn"], lw["w_gate"], lw["w_up"], lw["w_down"])


def _combine_kernel(final, tm, slots, starts_ref, ye_hbm, x_ref, pos_ref, w_ref, gt_ref, gf_ref, o_ref,
                    stage, stage_x, acc_ref, sem, sem_x):
    i = pl.program_id(0)
    last = pl.num_programs(0) - 1
    n_e, wn = N_EXPERTS, COMBINE_W
    per_pass = MXU_DIM // wn
    top = n_e * slots - wn
    slot = i % 2
    nslot = 1 - slot

    def window(tile, e, c):
        s_lo = starts_ref[tile * n_e + e]
        nominal = e * slots + (s_lo // BF16_ROWS) * BF16_ROWS + c * wn
        return jnp.minimum(nominal, top), nominal

    def win_copy(tile, e, c, dst, s):
        base, _ = window(tile, e, c)
        return pltpu.make_async_copy(ye_hbm.at[pl.ds(pl.multiple_of(base, BF16_ROWS), wn), :],
                                     dst.at[pl.ds(e * wn, wn), :], s)

    @pl.when(i == 0)
    def _():
        for e in range(n_e):
            win_copy(0, e, 0, stage.at[0], sem.at[0]).start()

    pltpu.make_async_copy(ye_hbm.at[pl.ds(0, n_e * wn), :], stage.at[slot], sem.at[slot]).wait()
    nxt = jnp.minimum(i + 1, last)
    for e in range(n_e):
        win_copy(nxt, e, 0, stage.at[nslot], sem.at[nslot]).start()

    pos = pos_ref[...]
    w = w_ref[...]
    lane = lax.broadcasted_iota(jnp.int32, (1, MXU_DIM), 1)

    def accumulate(src, c, first):
        for g in range(n_e // per_pass):
            tgt = val = None
            for j in range(per_pass):
                e = g * per_pass + j
                base, nominal = window(i, e, c)
                gs = pos[:, e:e + 1] + e * slots
                t_j = gs - (base - j * wn)
                in_round = (gs >= nominal) & (gs < nominal + wn)
                v_j = jnp.where(in_round, w[:, e:e + 1], 0.0)
                if j == 0:
                    tgt, val = t_j, v_j
                else:
                    here = lane >= j * wn
                    tgt = jnp.where(here, t_j, tgt)
                    val = jnp.where(here, v_j, val)
            place = jnp.where(tgt == lane, val, 0.0).astype(BF16)
            part = _dot(place, src[g * MXU_DIM:(g + 1) * MXU_DIM, :])
            if first and g == 0:
                acc_ref[...] = part
            else:
                acc_ref[...] += part

    accumulate(stage.at[slot], 0, True)

    rounds = 1
    for e in range(n_e):
        s_lo = starts_ref[i * n_e + e]
        span = starts_ref[(i + 1) * n_e + e] - (s_lo // BF16_ROWS) * BF16_ROWS
        rounds = jnp.maximum(rounds, (span + wn - 1) // wn)

    def extra_round(c, carry):
        for e in range(n_e):
            win_copy(i, e, c, stage_x, sem_x).start()
        pltpu.make_async_copy(ye_hbm.at[pl.ds(0, n_e * wn), :], stage_x, sem_x).wait()
        accumulate(stage_x, c, False)
        return carry

    lax.fori_loop(1, rounds, extra_round, 0)

    y = x_ref[...] + gt_ref[...] * acc_ref[...]
    if final:
        y = _rms(y) * gf_ref[...]
    o_ref[...] = y

    @pl.when(i == last)
    def _():
        pltpu.make_async_copy(ye_hbm.at[pl.ds(0, n_e * wn), :], stage.at[nslot], sem.at[nslot]).wait()


def _combine(starts, ye, x1, pos, w, modr, mod_row, g_final, *, slots, final):
    n, d = x1.shape
    tm = COMBINE_TM
    assert MXU_DIM % COMBINE_W == 0 and N_EXPERTS % (MXU_DIM // COMBINE_W) == 0
    grid_spec = pltpu.PrefetchScalarGridSpec(
        num_scalar_prefetch=1,
        grid=(n // tm,),
        in_specs=[
            pl.BlockSpec(memory_space=pl.ANY),
            pl.BlockSpec((tm, d), lambda i, *_: (i, 0)),
            pl.BlockSpec((tm, LANES), lambda i, *_: (i, 0)),
            pl.BlockSpec((tm, LANES), lambda i, *_: (i, 0)),
            pl.BlockSpec((None, 1, d), lambda i, *_: (mod_row(i * tm) * 6 + 5, 0, 0)),
            pl.BlockSpec((1, d), lambda i, *_: (0, 0)),
        ],
        out_specs=pl.BlockSpec((tm, d), lambda i, *_: (i, 0)),
        scratch_shapes=[pltpu.VMEM((2, N_EXPERTS * COMBINE_W, d), BF16),
                        pltpu.VMEM((N_EXPERTS * COMBINE_W, d), BF16),
                        pltpu.VMEM((tm, d), F32),
                        pltpu.SemaphoreType.DMA((2,)), pltpu.SemaphoreType.DMA(())],
    )
    return pl.pallas_call(
        functools.partial(_combine_kernel, final, tm, slots),
        grid_spec=grid_spec,
        out_shape=jax.ShapeDtypeStruct((n, d), F32),
        compiler_params=_params(("arbitrary",)),
        name="combine_final" if final else "combine",
    )(starts, ye, x1, pos, w, modr, g_final)


def _rope_tables(n_tokens, rot_dim):
    t = jnp.arange(n_tokens)
    row = (t // GRID_W).astype(F32)
    col = (t % GRID_W).astype(F32)
    quarter = rot_dim // 4
    inv = ROPE_THETA ** (-jnp.arange(quarter, dtype=F32) / quarter)
    ar, ac = row[:, None] * inv, col[:, None] * inv
    cos = jnp.concatenate([jnp.cos(ar), jnp.cos(ar), jnp.cos(ac), jnp.cos(ac)], axis=-1)
    sin = jnp.concatenate([-jnp.sin(ar), jnp.sin(ar), -jnp.sin(ac), jnp.sin(ac)], axis=-1)
    pad = LANES - rot_dim
    if pad:
        cos = jnp.concatenate([cos, jnp.ones((n_tokens, pad), F32)], axis=-1)
        sin = jnp.concatenate([sin, jnp.zeros((n_tokens, pad), F32)], axis=-1)
    return cos, sin


def _cast_kernel(*refs):
    n = len(refs) // 2
    for src, dst in zip(refs[:n], refs[n:]):
        dst[...] = src[...].astype(BF16)


def _cast_experts(*ws):
    n_l, n_e = ws[0].shape[:2]
    spec = lambda w: pl.BlockSpec((None, None, w.shape[2] // 2, w.shape[3]),
                                  lambda i, h: (i // n_e, i % n_e, h, 0))
    return pl.pallas_call(
        _cast_kernel,
        grid=(n_l * n_e, 2),
        in_specs=[spec(w) for w in ws],
        out_specs=[spec(w) for w in ws],
        out_shape=[jax.ShapeDtypeStruct(w.shape, BF16) for w in ws],
        compiler_params=_params(("arbitrary", "arbitrary")),
        name="cast_experts",
    )(*ws)


def _repack_kernel(o_kr, o_gl, w_ref, wa_ref, wgl_ref):
    rows = w_ref.shape[0]
    wa_ref[:, :C_KR] = w_ref[:, :C_KR].astype(BF16)
    wa_ref[:, C_KR:C_GQ] = jnp.concatenate(
        [w_ref[:, C_KR:o_kr], jnp.zeros((rows, C_GQ - C_KR - MLA_ROPE), F32)], axis=1).astype(BF16)
    wa_ref[:, C_GQ:] = w_ref[:, o_kr:o_gl].astype(BF16)
    wgl_ref[...] = w_ref[:, o_gl:].astype(BF16)


def _repack_w_in(w_in, l):
    _, d, cols = w_in.shape
    o_kr = MLA_Q_RANK + MLA_KV_RANK + MLA_ROPE
    o_gl = o_kr + (GQA_HEADS + 2 * GQA_KV_HEADS) * GQA_HEAD_DIM
    tm = 256
    return pl.pallas_call(
        functools.partial(_repack_kernel, o_kr, o_gl),
        grid=(d // tm,),
        in_specs=[pl.BlockSpec((None, tm, cols), lambda i: (l, i, 0))],
        out_specs=[pl.BlockSpec((tm, C_END), lambda i: (i, 0)),
                   pl.BlockSpec((tm, cols - o_gl), lambda i: (i, 0))],
        out_shape=[jax.ShapeDtypeStruct((d, C_END), BF16), jax.ShapeDtypeStruct((d, cols - o_gl), BF16)],
        compiler_params=_params(("arbitrary",)),
        name="repack_w_in",
    )(w_in)


def _layer_weights(l, g_attn_norm, w_in, b_gate, g_mla_q, w_mla_uq, g_mla_kv, w_mla_uk, w_mla_uv,
                   g_gqa_q, g_gqa_k, w_out, g_ffn_norm, w_router, w_gate_e, w_up_e, w_down_e):
    w_a, w_gl = _repack_w_in(w_in, l)
    w_uq = w_mla_uq[l].reshape(MLA_Q_RANK, MLA_HEADS, MLA_QK)
    w_uq = jnp.pad(w_uq, ((0, 0), (0, 0), (0, MLA_QK_PAD - MLA_QK))).reshape(MLA_Q_RANK, -1).astype(BF16)
    wr = w_router[l].T
    wr_hi = wr.astype(BF16)
    wr_lo = (wr - wr_hi.astype(F32)).astype(BF16)
    return {
        "g_attn": g_attn_norm[l][None], "g_ffn": g_ffn_norm[l][None], "w_a": w_a,
        "g_mla_q": g_mla_q[l][None], "w_uq": w_uq, "g_mla_kv": g_mla_kv[l][None],
        "g_gqa_q": g_gqa_q[l][None], "g_gqa_k": g_gqa_k[l][None],
        "w_uk": w_mla_uk[l].astype(BF16), "w_uv": w_mla_uv[l].astype(BF16),
        "w_gl": w_gl, "b_gate": b_gate[l][None], "w_out": w_out[l].astype(BF16),
        "wr": jnp.concatenate([wr_hi, wr_lo], axis=0),
        "w_gate": w_gate_e, "w_up": w_up_e, "w_down": w_down_e,
    }


def _combine_starts(pos, slot_end):
    starts = jnp.concatenate([pos[::COMBINE_TM, :N_EXPERTS],
                              jnp.full((1, N_EXPERTS), slot_end, jnp.int32)], axis=0)
    return starts.reshape(-1)


def kernel(x_prompt, x_sample, c, cache_mla_ckv, cache_mla_krope, cache_gqa_k, cache_gqa_v, c_ctx,
           w_ada, b_ada, g_attn_norm, w_in, b_gate, g_mla_q, w_mla_uq, g_mla_kv, w_mla_uk, w_mla_uv,
           g_gqa_q, g_gqa_k, w_out, g_ffn_norm, w_router, w_gate_e, w_up_e, w_down_e, g_final):
    bp, tp, d = x_prompt.shape
    bs, tl, _ = x_sample.shape
    n_layers = w_ada.shape[0]
    np_, ns = bp * tp, bs * tl
    kvw = GQA_KV_HEADS * GQA_HEAD_DIM
    cap_p = EC_FACTOR * tp // N_EXPERTS
    cap_s = EC_FACTOR * tl // N_EXPERTS
    ts = cap_s
    assert bp * cap_p == ts and bs + 1 <= 8
    n_tiles = 1 + bs
    slots = n_tiles * ts

    cc = jnp.zeros((8, d), F32).at[0].set(c_ctx).at[1:1 + bs].set(c)
    modr = _ada(cc, w_ada, b_ada).reshape(n_layers * 8 * 6, 1, d)

    cos_m, sin_m = _rope_tables(tl, MLA_ROPE)
    cos_g, sin_g = _rope_tables(tl, GQA_HEAD_DIM)
    tabs = (cos_m, sin_m, cos_g, sin_g)
    g_fin = g_final[None]

    w_gate_e, w_up_e, w_down_e = _cast_experts(w_gate_e, w_up_e, w_down_e)
    xp = x_prompt.reshape(np_, d)
    xs = x_sample.reshape(ns, d)
    st = {"ckv": [], "kr": [], "k": [], "v": []}
    for l in range(n_layers):
        lw = _layer_weights(l, g_attn_norm, w_in, b_gate, g_mla_q, w_mla_uq, g_mla_kv, w_mla_uk,
                            w_mla_uv, g_gqa_q, g_gqa_k, w_out, g_ffn_norm, w_router, w_gate_e,
                            w_up_e, w_down_e)
        row_p = lambda r, l=l: l * 8
        row_s = lambda r, l=l: l * 8 + 1 + r // tl

        qm, ckv, kr, qg, kg, vg, ckv32, kr32, kg32, vg32 = _qkv(
            xp, tp, modr, row_p, lw, tabs, rope=False, cache_out=True)
        st["ckv"].append(ckv32.reshape(bp, tp, MLA_KV_RANK))
        st["kr"].append(kr32.reshape(bp, tp, MLA_ROPE))
        st["k"].append(kg32.reshape(bp, tp, GQA_KV_HEADS, GQA_HEAD_DIM))
        st["v"].append(vg32.reshape(bp, tp, GQA_KV_HEADS, GQA_HEAD_DIM))
        k_m, v_m = _kv_expand(ckv.reshape(bp, tp, -1), kr.reshape(bp, tp, -1), lw["w_uk"], lw["w_uv"])
        oa_p = _attn(qm, k_m, v_m, hps=MLA_HEADS, rolled=False)
        ob_p = _attn(qg.reshape(bp, GQA_KV_HEADS, GQA_GROUP * tp, GQA_HEAD_DIM), kg.reshape(bp, tp, -1),
                     vg.reshape(bp, tp, -1), hps=GQA_KV_HEADS, rolled=False)
        x1p, aff_p = _out_proj(xp, tp, oa_p, ob_p.reshape(bp, GQA_HEADS, tp, HEAD_V), modr, row_p, lw)

        qm, ckv, kr, qg, kg, vg = _qkv(xs, tl, modr, row_s, lw, tabs, rope=True, cache_out=False)
        kr_cache = jnp.pad(cache_mla_krope[:, l], ((0, 0), (0, 0), (0, LANES - MLA_ROPE))).astype(BF16)
        ckv_all = jnp.concatenate([cache_mla_ckv[:, l].astype(BF16), ckv.reshape(bs, tl, -1)], axis=1)
        kr_all = jnp.concatenate([kr_cache, kr.reshape(bs, tl, -1)], axis=1)
        k_m, v_m = _kv_expand(ckv_all, kr_all, lw["w_uk"], lw["w_uv"])
        oa_s = _attn(qm, k_m, v_m, hps=1, rolled=True)
        past = cache_gqa_k.shape[2]
        k_all = jnp.concatenate([cache_gqa_k[:, l].reshape(bs, past, kvw).astype(BF16),
                                 kg.reshape(bs, tl, kvw)], axis=1)
        v_all = jnp.concatenate([cache_gqa_v[:, l].reshape(bs, past, kvw).astype(BF16),
                                 vg.reshape(bs, tl, kvw)], axis=1)
        ob_s = _attn(qg.reshape(bs, GQA_KV_HEADS, GQA_GROUP * tl, GQA_HEAD_DIM), k_all, v_all,
                     hps=1, rolled=True)
        x1s, aff_s = _out_proj(xs, tl, oa_s, ob_s.reshape(bs, GQA_HEADS, tl, HEAD_V), modr, row_s, lw)

        idx_p, w_p, pos_p = _router(aff_p, bp, 0)
        idx_s, w_s, pos_s = _router(aff_s, bs, ts)
        idx = jnp.concatenate([idx_p[:, :N_EXPERTS], idx_s[:, :N_EXPERTS]], axis=0).T.reshape(-1)
        ye = _ffn(idx, x1p, x1s, modr, l, lw, n_tiles, ts).reshape(N_EXPERTS * slots, d)
        final = l == n_layers - 1
        xp = _combine(_combine_starts(pos_p, ts), ye, x1p, pos_p, w_p, modr, row_p, g_fin,
                      slots=slots, final=final)
        xs = _combine(_combine_starts(pos_s, slots), ye, x1s, pos_s, w_s, modr, row_s, g_fin,
                      slots=slots, final=final)

    return (xp.reshape(bp, tp, d), xs.reshape(bs, tl, d),
            jnp.stack(st["ckv"], axis=1), jnp.stack(st["kr"], axis=1),
            jnp.stack(st["k"], axis=1), jnp.stack(st["v"], axis=1))
```
